```python
import math
import jax, jax.numpy as jnp
from jax import lax
import numpy as np

D_MODEL = 2048
BATCH = 4
SEQ = 4096
DEPTH = 4

N_A = DEPTH // 2
N_B = DEPTH - N_A
MEM_LEN = 256
HEAD_DIM = 128
MEM_HEADS = 4
MEM_W = MEM_HEADS * HEAD_DIM
MIX_W = D_MODEL - MEM_W
CHUNK = 128
GMLP_GROUPS = 6
GMLP_GW = MIX_W // GMLP_GROUPS
DIFF_HEADS = MIX_W // (2 * HEAD_DIM)
DIFF_VDIM = 2 * HEAD_DIM
D_FF = 5632
ROPE_THETA = 10000.0
EPS = 1e-6
Q_BLOCK = 128

kernel_name = "yoco_gmlp_diffattn_macaron_memxattn"


def rmsnorm(x, g):
    x32 = x.astype(jnp.float32)
    y = x32 * lax.rsqrt(jnp.mean(x32 * x32, axis=-1, keepdims=True) + EPS)
    return (y * g.astype(jnp.float32)).astype(x.dtype)


def rope_tables(seq):
    pos = jnp.arange(seq, dtype=jnp.float32)
    inv = ROPE_THETA ** (-jnp.arange(0, HEAD_DIM, 2, dtype=jnp.float32) / HEAD_DIM)
    ang = pos[:, None] * inv[None, :]
    ang = jnp.concatenate([ang, ang], axis=-1)
    return jnp.cos(ang), jnp.sin(ang)


def apply_rope(x, cos, sin):
    x1, x2 = jnp.split(x, 2, axis=-1)
    rot = jnp.concatenate([-x2, x1], axis=-1)
    return (x * cos + rot * sin).astype(x.dtype)


def swiglu(h, w_gu, w_down):
    g, u = jnp.split(h @ w_gu, 2, axis=-1)
    return (jax.nn.silu(g) * u) @ w_down


def gmlp_mix(z, v_norm_g, w_s, b_s):
    B, S, _ = z.shape
    u, v = jnp.split(z, 2, axis=-1)
    v = rmsnorm(v, v_norm_g)
    v = v.reshape(B, S // CHUNK, CHUNK, GMLP_GROUPS, GMLP_GW)
    causal = jnp.tril(jnp.ones((CHUNK, CHUNK), dtype=bool))
    w = jnp.where(causal[None], w_s, jnp.zeros_like(w_s))
    mixed = jnp.einsum('gts,bcsgd->bctgd', w, v) + jnp.transpose(b_s)[None, None, :, :, None]
    return u * mixed.reshape(B, S, MIX_W)


def diff_attention(q, k, v, lam, lam_init, subln_g):
    B, _, H, S, d = q.shape
    nb = S // Q_BLOCK
    qb = q.reshape(B, 2, H, nb, Q_BLOCK, d).transpose(3, 0, 1, 2, 4, 5)
    starts = jnp.arange(nb, dtype=jnp.int32) * Q_BLOCK
    kpos = jnp.arange(S, dtype=jnp.int32)
    scale = HEAD_DIM ** -0.5

    def block(args):
        qi, s0 = args
        s = jnp.einsum('bchqd,bchkd->bchqk', qi, k).astype(jnp.float32) * scale
        qpos = s0 + jnp.arange(Q_BLOCK, dtype=jnp.int32)
        mask = kpos[None, :] <= qpos[:, None]
        s = jnp.where(mask, s, -jnp.inf)
        p = jax.nn.softmax(s, axis=-1)
        a = p[:, 0] - lam * p[:, 1]
        return jnp.einsum('bhqk,bhkd->bhqd', a.astype(v.dtype), v)

    o = lax.map(block, (qb, starts))
    o = o.transpose(1, 0, 3, 2, 4).reshape(B, S, H, DIFF_VDIM)
    o = rmsnorm(o, subln_g) * (1.0 - lam_init)
    return o.reshape(B, S, MIX_W)


def mem_attention(qm, mk, mv):
    B, H, S, d = qm.shape
    s = jnp.einsum('bhqd,bhkd->bhqk', qm, mk).astype(jnp.float32) * (HEAD_DIM ** -0.5)
    p = jax.nn.softmax(s, axis=-1)
    o = jnp.einsum('bhqk,bhkd->bhqd', p.astype(mv.dtype), mv)
    return o.transpose(0, 2, 1, 3).reshape(B, S, MEM_W)


def setup_inputs(seed: int = 0) -> dict:
    key = jax.random.key(seed)
    ks = jax.random.split(key, 24)

    def nrm(k, shape, scale):
        return jax.random.normal(k, shape, jnp.float32) * scale

    def gain(k, shape):
        return 1.0 + 0.02 * jax.random.normal(k, shape, jnp.float32)

    return {
        "x": nrm(ks[0], (BATCH, SEQ, D_MODEL), 1.0),
        "mem": nrm(ks[1], (BATCH, MEM_LEN, D_MODEL), 1.0),
        "norm_g": gain(ks[2], (DEPTH, 3, D_MODEL)),
        "ffn_w_gu": nrm(ks[3], (DEPTH, 2, D_MODEL, 2 * D_FF), D_MODEL ** -0.5),
        "ffn_w_down": nrm(ks[4], (DEPTH, 2, D_FF, D_MODEL), D_FF ** -0.5),
        "w_out": nrm(ks[5], (DEPTH, D_MODEL, D_MODEL), D_MODEL ** -0.5),
        "mem_norm_g": gain(ks[6], (DEPTH, D_MODEL)),
        "mem_w_kv": nrm(ks[7], (DEPTH, D_MODEL, 2 * MEM_W), D_MODEL ** -0.5),
        "mem_q_norm_g": gain(ks[8], (DEPTH, HEAD_DIM)),
        "mem_k_norm_g": gain(ks[9], (DEPTH, HEAD_DIM)),
        "a_w_in": nrm(ks[10], (N_A, D_MODEL, 2 * MIX_W + MEM_W), D_MODEL ** -0.5),
        "a_v_norm_g": gain(ks[11], (N_A, MIX_W)),
        "a_w_s": nrm(ks[12], (N_A, GMLP_GROUPS, CHUNK, CHUNK), CHUNK ** -0.5),
        "a_b_s": gain(ks[13], (N_A, GMLP_GROUPS, CHUNK)),
        "kv_norm_g": gain(ks[14], (D_MODEL,)),
        "w_kv": nrm(ks[15], (D_MODEL, 2 * DIFF_HEADS * HEAD_DIM + DIFF_HEADS * DIFF_VDIM), D_MODEL ** -0.5),
        "k_norm_g": gain(ks[16], (HEAD_DIM,)),
        "b_w_in": nrm(ks[17], (N_B, D_MODEL, MIX_W + MEM_W), D_MODEL ** -0.5),
        "b_q_norm_g": gain(ks[18], (N_B, HEAD_DIM)),
        "b_lambda": nrm(ks[19], (N_B, 4, HEAD_DIM), 0.1),
        "b_subln_g": gain(ks[20], (N_B, DIFF_VDIM)),
    }


def reference(x, mem, norm_g, ffn_w_gu, ffn_w_down, w_out, mem_norm_g, mem_w_kv,
              mem_q_norm_g, mem_k_norm_g, a_w_in, a_v_norm_g, a_w_s, a_b_s,
              kv_norm_g, w_kv, k_norm_g, b_w_in, b_q_norm_g, b_lambda, b_subln_g):
    B, S, _ = x.shape
    M = mem.shape[1]
    cos, sin = rope_tables(S)
    HQK = DIFF_HEADS * HEAD_DIM
    k_sh = None
    v_sh = None
    for l in range(DEPTH):
        if l == N_A:
            kv = rmsnorm(x, kv_norm_g) @ w_kv
            k_sh = kv[..., :2 * HQK].reshape(B, S, 2, DIFF_HEADS, HEAD_DIM).transpose(0, 2, 3, 1, 4)
            k_sh = apply_rope(rmsnorm(k_sh, k_norm_g), cos, sin)
            v_sh = kv[..., 2 * HQK:].reshape(B, S, DIFF_HEADS, DIFF_VDIM).transpose(0, 2, 1, 3)

        x = x + 0.5 * swiglu(rmsnorm(x, norm_g[l, 0]), ffn_w_gu[l, 0], ffn_w_down[l, 0])

        mkv = rmsnorm(mem, mem_norm_g[l]) @ mem_w_kv[l]
        mk = mkv[..., :MEM_W].reshape(B, M, MEM_HEADS, HEAD_DIM).transpose(0, 2, 1, 3)
        mk = rmsnorm(mk, mem_k_norm_g[l])
        mv = mkv[..., MEM_W:].reshape(B, M, MEM_HEADS, HEAD_DIM).transpose(0, 2, 1, 3)

        h = rmsnorm(x, norm_g[l, 1])
        if l < N_A:
            z = h @ a_w_in[l]
            mix = gmlp_mix(jax.nn.gelu(z[..., :2 * MIX_W], approximate=False),
                           a_v_norm_g[l], a_w_s[l], a_b_s[l])
            qm = z[..., 2 * MIX_W:]
        else:
            j = l - N_A
            z = h @ b_w_in[j]
            q = z[..., :MIX_W].reshape(B, S, 2, DIFF_HEADS, HEAD_DIM).transpose(0, 2, 3, 1, 4)
            q = apply_rope(rmsnorm(q, b_q_norm_g[j]), cos, sin)
            lam_init = 0.8 - 0.6 * math.exp(-0.3 * l)
            lp = b_lambda[j].astype(jnp.float32)
            lam = jnp.exp(jnp.sum(lp[0] * lp[1])) - jnp.exp(jnp.sum(lp[2] * lp[3])) + lam_init
            mix = diff_attention(q, k_sh, v_sh, lam, lam_init, b_subln_g[j])
            qm = z[..., MIX_W:]
        qm = rmsnorm(qm.reshape(B, S, MEM_HEADS, HEAD_DIM), mem_q_norm_g[l]).transpose(0, 2, 1, 3)
        mo = mem_attention(qm, mk, mv)
        x = x + jnp.concatenate([mix, mo], axis=-1) @ w_out[l]

        x = x + 0.5 * swiglu(rmsnorm(x, norm_g[l, 2]), ffn_w_gu[l, 1], ffn_w_down[l, 1])
    return x
```

```python
import functools
import math

import jax
import jax.numpy as jnp
from jax import lax
from jax.experimental import pallas as pl
from jax.experimental.pallas import tpu as pltpu

EPS = 1e-6
HEAD_DIM = 128
MEM_HEADS = 4
MEM_W = MEM_HEADS * HEAD_DIM
CHUNK = 128
GMLP_GROUPS = 6
ROPE_THETA = 10000.0
ATTN_SCALE = HEAD_DIM ** -0.5
SQRT_HALF = 0.7071067811865476

V7X_VMEM_BYTES = 64 * 1024 * 1024
VMEM_LIMIT_BYTES = V7X_VMEM_BYTES - 8 * 1024 * 1024

FFN_TM = 512
FFN_TF = 512
PROJ_TM = 256
PROJ_TN = 512
OUT_TM = 512
ATTN_T = 256

F32 = jnp.float32
BF16 = jnp.bfloat16
NT_DIMS = (((1,), (1,)), ((), ()))


def _params(*semantics):
    return pltpu.CompilerParams(dimension_semantics=semantics,
                                vmem_limit_bytes=VMEM_LIMIT_BYTES)


def _rms(x, g):
    return x * lax.rsqrt(jnp.mean(x * x, axis=-1, keepdims=True) + EPS) * g


def _gelu(z):
    return 0.5 * z * (1.0 + lax.erf(z * SQRT_HALF))


def _rope(r, cos, sin_signed):
    return r * cos + pltpu.roll(r, HEAD_DIM // 2, 1) * sin_signed


def _mem_attn(zq, gq, mk_ref, mv_ref, mo_ref):
    for hh in range(MEM_HEADS):
        cols = slice(hh * HEAD_DIM, (hh + 1) * HEAD_DIM)
        qh = _rms(zq[:, cols], gq).astype(BF16)
        s = lax.dot_general(qh, mk_ref[:, cols], NT_DIMS, preferred_element_type=F32) * ATTN_SCALE
        e = jnp.exp(s - jnp.max(s, axis=-1, keepdims=True))
        p = e / jnp.sum(e, axis=-1, keepdims=True)
        mo_ref[:, cols] = jnp.dot(p.astype(BF16), mv_ref[:, cols],
                                  preferred_element_type=F32).astype(BF16)


def _ffn_kernel(x_ref, g_ref, wg_ref, wu_ref, wd_ref, o_ref, h_ref):
    @pl.when(pl.program_id(1) == 0)
    def _():
        x = x_ref[...]
        h_ref[...] = _rms(x, g_ref[...]).astype(BF16)
        o_ref[...] = x

    h = h_ref[...]
    g = jnp.dot(h, wg_ref[...], preferred_element_type=F32)
    u = jnp.dot(h, wu_ref[...], preferred_element_type=F32)
    act = (0.5 * g * jax.nn.sigmoid(g)) * u
    o_ref[...] += jnp.dot(act.astype(BF16), wd_ref[...], preferred_element_type=F32)


def _ffn(x, norm_g, w_gu, w_down, l, slot, i):
    n, d = x.shape
    f = w_down.shape[2]
    tm, tf = FFN_TM, FFN_TF
    nf = f // tf
    return pl.pallas_call(
        _ffn_kernel,
        grid=(n // tm, nf),
        in_specs=[
            pl.BlockSpec((tm, d), lambda r, j: (r, 0)),
            pl.BlockSpec((None, None, 1, d), lambda r, j: (l, slot, 0, 0)),
            pl.BlockSpec((None, None, d, tf), lambda r, j: (l, i, 0, j)),
            pl.BlockSpec((None, None, d, tf), lambda r, j: (l, i, 0, j + nf)),
            pl.BlockSpec((None, None, tf, d), lambda r, j: (l, i, j, 0)),
        ],
        out_specs=pl.BlockSpec((tm, d), lambda r, j: (r, 0)),
        out_shape=jax.ShapeDtypeStruct((n, d), F32),
        scratch_shapes=[pltpu.VMEM((tm, d), BF16)],
        compiler_params=_params("parallel", "arbitrary"),
        name="ffn",
    )(x, norm_g, w_gu, w_gu, w_down)


def _mem_kv_kernel(mem_ref, g_ref, w_ref, gk_ref, mk_ref, mv_ref):
    h = _rms(mem_ref[...], g_ref[...]).astype(BF16)
    kv = jnp.dot(h, w_ref[...], preferred_element_type=F32)
    gk = gk_ref[...]
    for hh in range(MEM_HEADS):
        cols = slice(hh * HEAD_DIM, (hh + 1) * HEAD_DIM)
        mk_ref[:, cols] = _rms(kv[:, cols], gk).astype(BF16)
    mv_ref[...] = kv[:, MEM_W:].astype(BF16)


def _mem_kv(mem2, mem_norm_g, mem_w_kv, mem_k_norm_g):
    rows, d = mem2.shape
    depth = mem_w_kv.shape[0]
    out = jax.ShapeDtypeStruct((depth, rows, MEM_W), BF16)
    return pl.pallas_call(
        _mem_kv_kernel,
        grid=(depth,),
        in_specs=[
            pl.BlockSpec((rows, d), lambda l: (0, 0)),
            pl.BlockSpec((None, 1, d), lambda l: (l, 0, 0)),
            pl.BlockSpec((None, d, 2 * MEM_W), lambda l: (l, 0, 0)),
            pl.BlockSpec((None, 1, HEAD_DIM), lambda l: (l, 0, 0)),
        ],
        out_specs=[pl.BlockSpec((None, rows, MEM_W), lambda l: (l, 0, 0))] * 2,
        out_shape=[out, out],
        compiler_params=_params("parallel"),
        name="mem_kv",
    )(mem2, mem_norm_g, mem_w_kv, mem_k_norm_g)


def _mixer_a_kernel(x_ref, g_ref, w_ref, gv_ref, ws_ref, bs_ref, gq_ref, mk_ref, mv_ref,
                    mix_ref, mo_ref, h_ref, u_ref, v_ref):
    tm = x_ref.shape[0]
    mix_w = u_ref.shape[1]
    tn = PROJ_TN
    h_ref[...] = _rms(x_ref[...], g_ref[...]).astype(BF16)
    for c in range(mix_w // tn):
        cols = slice(c * tn, (c + 1) * tn)
        u_ref[:, cols] = _gelu(jnp.dot(h_ref[...], w_ref[:, cols], preferred_element_type=F32))
        v_ref[:, cols] = _gelu(jnp.dot(h_ref[...], w_ref[:, mix_w + c * tn:mix_w + (c + 1) * tn],
                                       preferred_element_type=F32))
    zq = jnp.dot(h_ref[...], w_ref[:, 2 * mix_w:], preferred_element_type=F32)
    _mem_attn(zq, gq_ref[...], mk_ref, mv_ref, mo_ref)

    gw = mix_w // GMLP_GROUPS
    causal = (lax.broadcasted_iota(jnp.int32, (CHUNK, CHUNK), 0)
              >= lax.broadcasted_iota(jnp.int32, (CHUNK, CHUNK), 1))
    w_s = [jnp.where(causal, ws_ref[grp], 0.0).astype(BF16) for grp in range(GMLP_GROUPS)]
    for t in range(tm // CHUNK):
        rows = slice(t * CHUNK, (t + 1) * CHUNK)
        vn = _rms(v_ref[rows, :], gv_ref[...]).astype(BF16)
        for grp in range(GMLP_GROUPS):
            cols = slice(grp * gw, (grp + 1) * gw)
            mixed = (jnp.dot(w_s[grp], vn[:, cols], preferred_element_type=F32)
                     + bs_ref[:, grp:grp + 1])
            mix_ref[rows, cols] = (u_ref[rows, cols] * mixed).astype(BF16)


def _mixer_a(x, norm_g, l, w_in, v_norm_g, w_s, b_s_t, mem_q_norm_g, mk, mv, seq):
    n, d = x.shape
    mix_w = v_norm_g.shape[-1]
    tm = PROJ_TM
    blocks_per_seq = seq // tm
    mem_len = mk.shape[1] // (n // seq)
    mem_spec = pl.BlockSpec((None, mem_len, MEM_W), lambda r: (l, r // blocks_per_seq, 0))
    return pl.pallas_call(
        _mixer_a_kernel,
        grid=(n // tm,),
        in_specs=[
            pl.BlockSpec((tm, d), lambda r: (r, 0)),
            pl.BlockSpec((None, None, 1, d), lambda r: (l, 1, 0, 0)),
            pl.BlockSpec((None, d, 2 * mix_w + MEM_W), lambda r: (l, 0, 0)),
            pl.BlockSpec((None, 1, mix_w), lambda r: (l, 0, 0)),
            pl.BlockSpec((None, GMLP_GROUPS, CHUNK, CHUNK), lambda r: (l, 0, 0, 0)),
            pl.BlockSpec((None, CHUNK, GMLP_GROUPS), lambda r: (l, 0, 0)),
            pl.BlockSpec((None, 1, HEAD_DIM), lambda r: (l, 0, 0)),
            mem_spec, mem_spec,
        ],
        out_specs=[pl.BlockSpec((tm, mix_w), lambda r: (r, 0)),
                   pl.BlockSpec((tm, MEM_W), lambda r: (r, 0))],
        out_shape=[jax.ShapeDtypeStruct((n, mix_w), BF16),
                   jax.ShapeDtypeStruct((n, MEM_W), BF16)],
        scratch_shapes=[pltpu.VMEM((tm, d), BF16),
                        pltpu.VMEM((tm, mix_w), F32),
                        pltpu.VMEM((tm, mix_w), F32)],
        compiler_params=_params("parallel"),
        name="mixer_a",
    )(x, norm_g, w_in, v_norm_g, w_s, b_s_t, mem_q_norm_g, mk, mv)


def _kv_proj_kernel(x_ref, g_ref, w_ref, gk_ref, cos_ref, sin_ref, k_ref, v_ref, h_ref):
    qk_w = k_ref.shape[1]
    tn = PROJ_TN
    h_ref[...] = _rms(x_ref[...], g_ref[...]).astype(BF16)
    gk, cos, sin = gk_ref[...], cos_ref[...], sin_ref[...]
    for c in range(qk_w // tn):
        z = jnp.dot(h_ref[...], w_ref[:, c * tn:(c + 1) * tn], preferred_element_type=F32)
        for k in range(tn // HEAD_DIM):
            r = _rms(z[:, k * HEAD_DIM:(k + 1) * HEAD_DIM], gk)
            k_ref[:, c * tn + k * HEAD_DIM:c * tn + (k + 1) * HEAD_DIM] = _rope(r, cos, sin).astype(BF16)
    for c in range(v_ref.shape[1] // tn):
        cols = slice(c * tn, (c + 1) * tn)
        v_ref[:, cols] = jnp.dot(h_ref[...], w_ref[:, qk_w + c * tn:qk_w + (c + 1) * tn],
                                 preferred_element_type=F32).astype(BF16)


def _kv_proj(x, kv_norm_g, w_kv, k_norm_g, cos, sin_signed, seq, qk_w):
    n, d = x.shape
    v_w = w_kv.shape[1] - qk_w
    tm = PROJ_TM
    blocks_per_seq = seq // tm
    rope_spec = pl.BlockSpec((tm, HEAD_DIM), lambda r: (r % blocks_per_seq, 0))
    return pl.pallas_call(
        _kv_proj_kernel,
        grid=(n // tm,),
        in_specs=[
            pl.BlockSpec((tm, d), lambda r: (r, 0)),
            pl.BlockSpec((1, d), lambda r: (0, 0)),
            pl.BlockSpec((d, qk_w + v_w), lambda r: (0, 0)),
            pl.BlockSpec((1, HEAD_DIM), lambda r: (0, 0)),
            rope_spec, rope_spec,
        ],
        out_specs=[pl.BlockSpec((tm, qk_w), lambda r: (r, 0)),
                   pl.BlockSpec((tm, v_w), lambda r: (r, 0))],
        out_shape=[jax.ShapeDtypeStruct((n, qk_w), BF16),
                   jax.ShapeDtypeStruct((n, v_w), BF16)],
        scratch_shapes=[pltpu.VMEM((tm, d), BF16)],
        compiler_params=_params("parallel"),
        name="kv_proj",
    )(x, kv_norm_g, w_kv, k_norm_g, cos, sin_signed)


def _proj_b_kernel(x_ref, g_ref, w_ref, gqn_ref, cos_ref, sin_ref, gq_ref, mk_ref, mv_ref,
                   q_ref, mo_ref, h_ref):
    qk_w = q_ref.shape[1]
    tn = PROJ_TN
    h_ref[...] = _rms(x_ref[...], g_ref[...]).astype(BF16)
    gqn, cos, sin = gqn_ref[...], cos_ref[...], sin_ref[...]
    for c in range(qk_w // tn):
        z = jnp.dot(h_ref[...], w_ref[:, c * tn:(c + 1) * tn], preferred_element_type=F32)
        for k in range(tn // HEAD_DIM):
            r = _rms(z[:, k * HEAD_DIM:(k + 1) * HEAD_DIM], gqn)
            q_ref[:, c * tn + k * HEAD_DIM:c * tn + (k + 1) * HEAD_DIM] = _rope(r, cos, sin).astype(BF16)
    zq = jnp.dot(h_ref[...], w_ref[:, qk_w:], preferred_element_type=F32)
    _mem_attn(zq, gq_ref[...], mk_ref, mv_ref, mo_ref)


def _proj_b(x, norm_g, l, w_in, j, q_norm_g, cos, sin_signed, mem_q_norm_g, mk, mv, seq):
    n, d = x.shape
    qk_w = w_in.shape[2] - MEM_W
    tm = PROJ_TM
    blocks_per_seq = seq // tm
    mem_len = mk.shape[1] // (n // seq)
    rope_spec = pl.BlockSpec((tm, HEAD_DIM), lambda r: (r % blocks_per_seq, 0))
    mem_spec = pl.BlockSpec((None, mem_len, MEM_W), lambda r: (l, r // blocks_per_seq, 0))
    return pl.pallas_call(
        _proj_b_kernel,
        grid=(n // tm,),
        in_specs=[
            pl.BlockSpec((tm, d), lambda r: (r, 0)),
            pl.BlockSpec((None, None, 1, d), lambda r: (l, 1, 0, 0)),
            pl.BlockSpec((None, d, qk_w + MEM_W), lambda r: (j, 0, 0)),
            pl.BlockSpec((None, 1, HEAD_DIM), lambda r: (j, 0, 0)),
            rope_spec, rope_spec,
            pl.BlockSpec((None, 1, HEAD_DIM), lambda r: (l, 0, 0)),
            mem_spec, mem_spec,
        ],
        out_specs=[pl.BlockSpec((tm, qk_w), lambda r: (r, 0)),
                   pl.BlockSpec((tm, MEM_W), lambda r: (r, 0))],
        out_shape=[jax.ShapeDtypeStruct((n, qk_w), BF16),
                   jax.ShapeDtypeStruct((n, MEM_W), BF16)],
        scratch_shapes=[pltpu.VMEM((tm, d), BF16)],
        compiler_params=_params("parallel"),
        name="proj_b",
    )(x, norm_g, w_in, q_norm_g, cos, sin_signed, mem_q_norm_g, mk, mv)


def _diff_attn_kernel(lam_ref, gs_ref, q0_ref, q1_ref, k0_ref, k1_ref, v_ref, o_ref,
                      m_ref, l_ref, acc_ref, *, lam_init):
    t = ATTN_T
    qi = pl.program_id(2)
    m_ref[...] = jnp.full(m_ref.shape, -jnp.inf, F32)
    l_ref[...] = jnp.zeros(l_ref.shape, F32)
    acc_ref[...] = jnp.zeros(acc_ref.shape, F32)
    causal = (lax.broadcasted_iota(jnp.int32, (t, t), 0)
              >= lax.broadcasted_iota(jnp.int32, (t, t), 1))

    def block(ki, masked):
        rows = pl.ds(pl.multiple_of(ki * t, t), t)
        v = v_ref[rows, :]
        for c, (q_ref, k_ref) in enumerate(((q0_ref, k0_ref), (q1_ref, k1_ref))):
            s = lax.dot_general(q_ref[...], k_ref[rows, :], NT_DIMS,
                                preferred_element_type=F32) * ATTN_SCALE
            if masked:
                s = jnp.where(causal, s, -jnp.inf)
            m_old = m_ref[c]
            m_new = jnp.maximum(m_old, jnp.max(s, axis=-1, keepdims=True))
            alpha = jnp.exp(m_old - m_new)
            p = jnp.exp(s - m_new)
            l_ref[c] = alpha * l_ref[c] + jnp.sum(p, axis=-1, keepdims=True)
            acc_ref[c] = alpha * acc_ref[c] + jnp.dot(p.astype(BF16), v, preferred_element_type=F32)
            m_ref[c] = m_new

    def body(ki, carry):
        block(ki, False)
        return carry

    lax.fori_loop(0, qi, body, 0)
    block(qi, True)

    lp = lam_ref[...]
    lam = (jnp.exp(jnp.sum(lp[0:1] * lp[1:2], axis=-1, keepdims=True))
           - jnp.exp(jnp.sum(lp[2:3] * lp[3:4], axis=-1, keepdims=True)) + lam_init)
    o = acc_ref[0] / l_ref[0] - lam * (acc_ref[1] / l_ref[1])
    o_ref[...] = (_rms(o, gs_ref[...]) * (1.0 - lam_init)).astype(BF16)


def _diff_attn(q, k, v, b_lambda, subln_g, j, lam_init, batch, seq):
    n, qk_w = q.shape
    heads = qk_w // (2 * HEAD_DIM)
    vd = v.shape[1] // heads
    t = ATTN_T
    q3 = q.reshape(batch, seq, qk_w)
    k3 = k.reshape(batch, seq, qk_w)
    v3 = v.reshape(batch, seq, heads * vd)
    o = pl.pallas_call(
        functools.partial(_diff_attn_kernel, lam_init=lam_init),
        grid=(batch, heads, seq // t),
        in_specs=[
            pl.BlockSpec((None, 4, HEAD_DIM), lambda b, h, i: (j, 0, 0)),
            pl.BlockSpec((None, 1, vd), lambda b, h, i: (j, 0, 0)),
            pl.BlockSpec((None, t, HEAD_DIM), lambda b, h, i: (b, i, h)),
            pl.BlockSpec((None, t, HEAD_DIM), lambda b, h, i: (b, i, heads + h)),
            pl.BlockSpec((None, seq, HEAD_DIM), lambda b, h, i: (b, 0, h)),
            pl.BlockSpec((None, seq, HEAD_DIM), lambda b, h, i: (b, 0, heads + h)),
            pl.BlockSpec((None, seq, vd), lambda b, h, i: (b, 0, h)),
        ],
        out_specs=pl.BlockSpec((None, t, vd), lambda b, h, i: (b, i, h)),
        out_shape=jax.ShapeDtypeStruct((batch, seq, heads * vd), BF16),
        scratch_shapes=[pltpu.VMEM((2, t, 1), F32),
                        pltpu.VMEM((2, t, 1), F32),
                        pltpu.VMEM((2, t, vd), F32)],
        compiler_params=_params("parallel", "parallel", "parallel"),
        name="diff_attn",
    )(b_lambda, subln_g, q3, q3, k3, k3, v3)
    return o.reshape(n, heads * vd)


def _out_proj_kernel(x_ref, mix_ref, mo_ref, w_ref, o_ref):
    mix_w = mix_ref.shape[1]
    tn = PROJ_TN
    for c in range(o_ref.shape[1] // tn):
        cols = slice(c * tn, (c + 1) * tn)
        y = (jnp.dot(mix_ref[...], w_ref[:mix_w, cols], preferred_element_type=F32)
             + jnp.dot(mo_ref[...], w_ref[mix_w:, cols], preferred_element_type=F32))
        o_ref[:, cols] = x_ref[:, cols] + y


def _out_proj(x, mix, mo, w_out, l):
    n, d = x.shape
    mix_w = mix.shape[1]
    tm = OUT_TM
    return pl.pallas_call(
        _out_proj_kernel,
        grid=(n // tm,),
        in_specs=[
            pl.BlockSpec((tm, d), lambda r: (r, 0)),
            pl.BlockSpec((tm, mix_w), lambda r: (r, 0)),
            pl.BlockSpec((tm, MEM_W), lambda r: (r, 0)),
            pl.BlockSpec((None, d, d), lambda r: (l, 0, 0)),
        ],
        out_specs=pl.BlockSpec((tm, d), lambda r: (r, 0)),
        out_shape=jax.ShapeDtypeStruct((n, d), F32),
        compiler_params=_params("parallel"),
        name="out_proj",
    )(x, mix, mo, w_out)


def kernel(x, mem, norm_g, ffn_w_gu, ffn_w_down, w_out, mem_norm_g, mem_w_kv, mem_q_norm_g,
           mem_k_norm_g, a_w_in, a_v_norm_g, a_w_s, a_b_s, kv_norm_g, w_kv, k_norm_g, b_w_in,
           b_q_norm_g, b_lambda, b_subln_g):
    batch, seq, d = x.shape
    depth = norm_g.shape[0]
    n_a = a_w_in.shape[0]
    mix_w = d - MEM_W
    n = batch * seq

    ffn_w_gu, ffn_w_down, w_out, mem_w_kv, a_w_in, w_kv, b_w_in = (
        w.astype(BF16) for w in (ffn_w_gu, ffn_w_down, w_out, mem_w_kv, a_w_in, w_kv, b_w_in))

    norm_g = norm_g[:, :, None, :]
    mem_norm_g = mem_norm_g[:, None, :]
    mem_q_norm_g = mem_q_norm_g[:, None, :]
    mem_k_norm_g = mem_k_norm_g[:, None, :]
    a_v_norm_g = a_v_norm_g[:, None, :]
    b_q_norm_g = b_q_norm_g[:, None, :]
    b_subln_g = b_subln_g[:, None, :]
    a_b_s_t = jnp.swapaxes(a_b_s, 1, 2)

    pos = jnp.arange(seq, dtype=F32)
    inv = ROPE_THETA ** (-jnp.arange(0, HEAD_DIM, 2, dtype=F32) / HEAD_DIM)
    ang = pos[:, None] * inv[None, :]
    ang = jnp.concatenate([ang, ang], axis=-1)
    cos = jnp.cos(ang)
    sign = jnp.where(jnp.arange(HEAD_DIM) < HEAD_DIM // 2, -1.0, 1.0).astype(F32)
    sin_signed = jnp.sin(ang) * sign[None, :]

    x = x.reshape(n, d)
    mk, mv = _mem_kv(mem.reshape(batch * mem.shape[1], d), mem_norm_g, mem_w_kv, mem_k_norm_g)

    k_sh = v_sh = None
    for l in range(depth):
        if l == n_a:
            k_sh, v_sh = _kv_proj(x, kv_norm_g[None, :], w_kv, k_norm_g[None, :], cos, sin_signed,
                                  seq, mix_w)
        x = _ffn(x, norm_g, ffn_w_gu, ffn_w_down, l, 0, 0)
        if l < n_a:
            mix, mo = _mixer_a(x, norm_g, l, a_w_in, a_v_norm_g, a_w_s, a_b_s_t, mem_q_norm_g,
                               mk, mv, seq)
        else:
            j = l - n_a
            q, mo = _proj_b(x, norm_g, l, b_w_in, j, b_q_norm_g, cos, sin_signed, mem_q_norm_g,
                            mk, mv, seq)
            lam_init = 0.8 - 0.6 * math.exp(-0.3 * l)
            mix = _diff_attn(q, k_sh, v_sh, b_lambda, b_subln_g, j, lam_init, batch, seq)
        x = _out_proj(x, mix, mo, w_out, l)
        x = _ffn(x, norm_g, ffn_w_gu, ffn_w_down, l, 2, 1)
    return x.reshape(batch, seq, d)
```

```python
import functools
import math

import jax
import jax.numpy as jnp
from jax import lax
from jax.experimental import pallas as pl
from jax.experimental.pallas import tpu as pltpu

EPS = 1e-6
HEAD_DIM = 128
MEM_HEADS = 4
MEM_W = MEM_HEADS * HEAD_DIM
CHUNK = 128
GMLP_GROUPS = 6
ROPE_THETA = 10000.0
ATTN_SCALE = HEAD_DIM ** -0.5
SQRT_HALF = 0.7071067811865476

V7X_VMEM_BYTES = 64 * 1024 * 1024
VMEM_LIMIT_BYTES = V7X_VMEM_BYTES - 8 * 1024 * 1024

FFN_TM = 512
FFN_TF = 512
PROJ_TM = 256
PROJ_TN = 512
OUT_TM = 512
ATTN_T = 512
LOG2E = 1.4426950408889634
Q_PRESCALE = ATTN_SCALE * LOG2E

F32 = jnp.float32
BF16 = jnp.bfloat16
NT_DIMS = (((1,), (1,)), ((), ()))


def _params(*semantics):
    return pltpu.CompilerParams(dimension_semantics=semantics,
                                vmem_limit_bytes=VMEM_LIMIT_BYTES)


def _rms(x, g):
    return x * lax.rsqrt(jnp.mean(x * x, axis=-1, keepdims=True) + EPS) * g


def _gelu(z):
    return 0.5 * z * (1.0 + lax.erf(z * SQRT_HALF))


def _rope(r, cos, sin_signed):
    return r * cos + pltpu.roll(r, HEAD_DIM // 2, 1) * sin_signed


def _mem_attn(zq, gq, mk_ref, mv_ref, mo_ref):
    for hh in range(MEM_HEADS):
        cols = slice(hh * HEAD_DIM, (hh + 1) * HEAD_DIM)
        qh = _rms(zq[:, cols], gq).astype(BF16)
        s = lax.dot_general(qh, mk_ref[:, cols], NT_DIMS, preferred_element_type=F32) * ATTN_SCALE
        e = jnp.exp(s - jnp.max(s, axis=-1, keepdims=True))
        p = e / jnp.sum(e, axis=-1, keepdims=True)
        mo_ref[:, cols] = jnp.dot(p.astype(BF16), mv_ref[:, cols],
                                  preferred_element_type=F32).astype(BF16)


def _ffn_kernel(x_ref, g_ref, wg_ref, wu_ref, wd_ref, o_ref, h_ref):
    @pl.when(pl.program_id(1) == 0)
    def _():
        x = x_ref[...]
        h_ref[...] = _rms(x, g_ref[...]).astype(BF16)
        o_ref[...] = x

    h = h_ref[...]
    g = jnp.dot(h, wg_ref[...], preferred_element_type=F32)
    u = jnp.dot(h, wu_ref[...], preferred_element_type=F32)
    act = (0.5 * g * jax.nn.sigmoid(g)) * u
    o_ref[...] += jnp.dot(act.astype(BF16), wd_ref[...], preferred_element_type=F32)


def _ffn(x, norm_g, w_gu, w_down, l, slot, i):
    n, d = x.shape
    f = w_down.shape[2]
    tm, tf = FFN_TM, FFN_TF
    nf = f // tf
    return pl.pallas_call(
        _ffn_kernel,
        grid=(n // tm, nf),
        in_specs=[
            pl.BlockSpec((tm, d), lambda r, j: (r, 0)),
            pl.BlockSpec((None, None, 1, d), lambda r, j: (l, slot, 0, 0)),
            pl.BlockSpec((None, None, d, tf), lambda r, j: (l, i, 0, j)),
            pl.BlockSpec((None, None, d, tf), lambda r, j: (l, i, 0, j + nf)),
            pl.BlockSpec((None, None, tf, d), lambda r, j: (l, i, j, 0)),
        ],
        out_specs=pl.BlockSpec((tm, d), lambda r, j: (r, 0)),
        out_shape=jax.ShapeDtypeStruct((n, d), F32),
        scratch_shapes=[pltpu.VMEM((tm, d), BF16)],
        compiler_params=_params("parallel", "arbitrary"),
        name="ffn",
    )(x, norm_g, w_gu, w_gu, w_down)


def _mem_kv_kernel(mem_ref, g_ref, w_ref, gk_ref, mk_ref, mv_ref):
    h = _rms(mem_ref[...], g_ref[...]).astype(BF16)
    kv = jnp.dot(h, w_ref[...], preferred_element_type=F32)
    gk = gk_ref[...]
    for hh in range(MEM_HEADS):
        cols = slice(hh * HEAD_DIM, (hh + 1) * HEAD_DIM)
        mk_ref[:, cols] = _rms(kv[:, cols], gk).astype(BF16)
    mv_ref[...] = kv[:, MEM_W:].astype(BF16)


def _mem_kv(mem2, mem_norm_g, mem_w_kv, mem_k_norm_g):
    rows, d = mem2.shape
    depth = mem_w_kv.shape[0]
    out = jax.ShapeDtypeStruct((depth, rows, MEM_W), BF16)
    return pl.pallas_call(
        _mem_kv_kernel,
        grid=(depth,),
        in_specs=[
            pl.BlockSpec((rows, d), lambda l: (0, 0)),
            pl.BlockSpec((None, 1, d), lambda l: (l, 0, 0)),
            pl.BlockSpec((None, d, 2 * MEM_W), lambda l: (l, 0, 0)),
            pl.BlockSpec((None, 1, HEAD_DIM), lambda l: (l, 0, 0)),
        ],
        out_specs=[pl.BlockSpec((None, rows, MEM_W), lambda l: (l, 0, 0))] * 2,
        out_shape=[out, out],
        compiler_params=_params("parallel"),
        name="mem_kv",
    )(mem2, mem_norm_g, mem_w_kv, mem_k_norm_g)


def _mixer_a_kernel(x_ref, g_ref, w_ref, gv_ref, ws_ref, bs_ref, gq_ref, mk_ref, mv_ref,
                    mix_ref, mo_ref, h_ref, u_ref, v_ref):
    tm = x_ref.shape[0]
    mix_w = u_ref.shape[1]
    tn = PROJ_TN
    h_ref[...] = _rms(x_ref[...], g_ref[...]).astype(BF16)
    for c in range(mix_w // tn):
        cols = slice(c * tn, (c + 1) * tn)
        u_ref[:, cols] = _gelu(jnp.dot(h_ref[...], w_ref[:, cols], preferred_element_type=F32))
        v_ref[:, cols] = _gelu(jnp.dot(h_ref[...], w_ref[:, mix_w + c * tn:mix_w + (c + 1) * tn],
                                       preferred_element_type=F32))
    zq = jnp.dot(h_ref[...], w_ref[:, 2 * mix_w:], preferred_element_type=F32)
    _mem_attn(zq, gq_ref[...], mk_ref, mv_ref, mo_ref)

    gw = mix_w // GMLP_GROUPS
    causal = (lax.broadcasted_iota(jnp.int32, (CHUNK, CHUNK), 0)
              >= lax.broadcasted_iota(jnp.int32, (CHUNK, CHUNK), 1))
    w_s = [jnp.where(causal, ws_ref[grp], 0.0).astype(BF16) for grp in range(GMLP_GROUPS)]
    for t in range(tm // CHUNK):
        rows = slice(t * CHUNK, (t + 1) * CHUNK)
        vn = _rms(v_ref[rows, :], gv_ref[...]).astype(BF16)
        for grp in range(GMLP_GROUPS):
            cols = slice(grp * gw, (grp + 1) * gw)
            mixed = (jnp.dot(w_s[grp], vn[:, cols], preferred_element_type=F32)
                     + bs_ref[:, grp:grp + 1])
            mix_ref[rows, cols] = (u_ref[rows, cols] * mixed).astype(BF16)


def _mixer_a(x, norm_g, l, w_in, v_norm_g, w_s, b_s_t, mem_q_norm_g, mk, mv, seq):
    n, d = x.shape
    mix_w = v_norm_g.shape[-1]
    tm = PROJ_TM
    blocks_per_seq = seq // tm
    mem_len = mk.shape[1] // (n // seq)
    mem_spec = pl.BlockSpec((None, mem_len, MEM_W), lambda r: (l, r // blocks_per_seq, 0))
    return pl.pallas_call(
        _mixer_a_kernel,
        grid=(n // tm,),
        in_specs=[
            pl.BlockSpec((tm, d), lambda r: (r, 0)),
            pl.BlockSpec((None, None, 1, d), lambda r: (l, 1, 0, 0)),
            pl.BlockSpec((None, d, 2 * mix_w + MEM_W), lambda r: (l, 0, 0)),
            pl.BlockSpec((None, 1, mix_w), lambda r: (l, 0, 0)),
            pl.BlockSpec((None, GMLP_GROUPS, CHUNK, CHUNK), lambda r: (l, 0, 0, 0)),
            pl.BlockSpec((None, CHUNK, GMLP_GROUPS), lambda r: (l, 0, 0)),
            pl.BlockSpec((None, 1, HEAD_DIM), lambda r: (l, 0, 0)),
            mem_spec, mem_spec,
        ],
        out_specs=[pl.BlockSpec((tm, mix_w), lambda r: (r, 0)),
                   pl.BlockSpec((tm, MEM_W), lambda r: (r, 0))],
        out_shape=[jax.ShapeDtypeStruct((n, mix_w), BF16),
                   jax.ShapeDtypeStruct((n, MEM_W), BF16)],
        scratch_shapes=[pltpu.VMEM((tm, d), BF16),
                        pltpu.VMEM((tm, mix_w), F32),
                        pltpu.VMEM((tm, mix_w), F32)],
        compiler_params=_params("parallel"),
        name="mixer_a",
    )(x, norm_g, w_in, v_norm_g, w_s, b_s_t, mem_q_norm_g, mk, mv)


def _kv_proj_kernel(x_ref, g_ref, w_ref, gk_ref, cos_ref, sin_ref, k_ref, vt_ref, h_ref, v_ref):
    qk_w = k_ref.shape[1]
    heads, vd, _ = vt_ref.shape
    tn = PROJ_TN
    h_ref[...] = _rms(x_ref[...], g_ref[...]).astype(BF16)
    gk, cos, sin = gk_ref[...], cos_ref[...], sin_ref[...]
    for c in range(qk_w // tn):
        z = jnp.dot(h_ref[...], w_ref[:, c * tn:(c + 1) * tn], preferred_element_type=F32)
        for k in range(tn // HEAD_DIM):
            r = _rms(z[:, k * HEAD_DIM:(k + 1) * HEAD_DIM], gk)
            k_ref[:, c * tn + k * HEAD_DIM:c * tn + (k + 1) * HEAD_DIM] = _rope(r, cos, sin).astype(BF16)
    for hh in range(heads):
        v_ref[...] = jnp.dot(h_ref[...], w_ref[:, qk_w + hh * vd:qk_w + (hh + 1) * vd],
                             preferred_element_type=F32)
        vt_ref[hh] = v_ref[...].T.astype(BF16)


def _kv_proj(x, kv_norm_g, w_kv, k_norm_g, cos, sin_signed, batch, seq, qk_w, vd):
    n, d = x.shape
    heads = (w_kv.shape[1] - qk_w) // vd
    tm = ATTN_T
    blocks_per_seq = seq // tm
    rope_spec = pl.BlockSpec((tm, HEAD_DIM), lambda r: (r % blocks_per_seq, 0))
    return pl.pallas_call(
        _kv_proj_kernel,
        grid=(n // tm,),
        in_specs=[
            pl.BlockSpec((tm, d), lambda r: (r, 0)),
            pl.BlockSpec((1, d), lambda r: (0, 0)),
            pl.BlockSpec((d, qk_w + heads * vd), lambda r: (0, 0)),
            pl.BlockSpec((1, HEAD_DIM), lambda r: (0, 0)),
            rope_spec, rope_spec,
        ],
        out_specs=[pl.BlockSpec((tm, qk_w), lambda r: (r, 0)),
                   pl.BlockSpec((None, None, heads, vd, tm),
                                lambda r: (r // blocks_per_seq, r % blocks_per_seq, 0, 0, 0))],
        out_shape=[jax.ShapeDtypeStruct((n, qk_w), BF16),
                   jax.ShapeDtypeStruct((batch, blocks_per_seq, heads, vd, tm), BF16)],
        scratch_shapes=[pltpu.VMEM((tm, d), BF16), pltpu.VMEM((tm, vd), F32)],
        compiler_params=_params("parallel"),
        name="kv_proj",
    )(x, kv_norm_g, w_kv, k_norm_g, cos, sin_signed)


def _proj_b_kernel(x_ref, g_ref, w_ref, gqn_ref, cos_ref, sin_ref, gq_ref, mk_ref, mv_ref,
                   q_ref, mo_ref, h_ref):
    qk_w = q_ref.shape[1]
    tn = PROJ_TN
    h_ref[...] = _rms(x_ref[...], g_ref[...]).astype(BF16)
    gqn, cos, sin = gqn_ref[...], cos_ref[...], sin_ref[...]
    for c in range(qk_w // tn):
        z = jnp.dot(h_ref[...], w_ref[:, c * tn:(c + 1) * tn], preferred_element_type=F32)
        for k in range(tn // HEAD_DIM):
            r = _rms(z[:, k * HEAD_DIM:(k + 1) * HEAD_DIM], gqn)
            q_ref[:, c * tn + k * HEAD_DIM:c * tn + (k + 1) * HEAD_DIM] = (
                _rope(r, cos, sin) * Q_PRESCALE).astype(BF16)
    zq = jnp.dot(h_ref[...], w_ref[:, qk_w:], preferred_element_type=F32)
    _mem_attn(zq, gq_ref[...], mk_ref, mv_ref, mo_ref)


def _proj_b(x, norm_g, l, w_in, j, q_norm_g, cos, sin_signed, mem_q_norm_g, mk, mv, seq):
    n, d = x.shape
    qk_w = w_in.shape[2] - MEM_W
    tm = PROJ_TM
    blocks_per_seq = seq // tm
    mem_len = mk.shape[1] // (n // seq)
    rope_spec = pl.BlockSpec((tm, HEAD_DIM), lambda r: (r % blocks_per_seq, 0))
    mem_spec = pl.BlockSpec((None, mem_len, MEM_W), lambda r: (l, r // blocks_per_seq, 0))
    return pl.pallas_call(
        _proj_b_kernel,
        grid=(n // tm,),
        in_specs=[
            pl.BlockSpec((tm, d), lambda r: (r, 0)),
            pl.BlockSpec((None, None, 1, d), lambda r: (l, 1, 0, 0)),
            pl.BlockSpec((None, d, qk_w + MEM_W), lambda r: (j, 0, 0)),
            pl.BlockSpec((None, 1, HEAD_DIM), lambda r: (j, 0, 0)),
            rope_spec, rope_spec,
            pl.BlockSpec((None, 1, HEAD_DIM), lambda r: (l, 0, 0)),
            mem_spec, mem_spec,
        ],
        out_specs=[pl.BlockSpec((tm, qk_w), lambda r: (r, 0)),
                   pl.BlockSpec((tm, MEM_W), lambda r: (r, 0))],
        out_shape=[jax.ShapeDtypeStruct((n, qk_w), BF16),
                   jax.ShapeDtypeStruct((n, MEM_W), BF16)],
        scratch_shapes=[pltpu.VMEM((tm, d), BF16)],
        compiler_params=_params("parallel"),
        name="proj_b",
    )(x, norm_g, w_in, q_norm_g, cos, sin_signed, mem_q_norm_g, mk, mv)


def _diff_attn_kernel(lam_ref, gs_ref, q0_ref, q1_ref, k0_ref, k1_ref, vt_ref, o_ref, acc_ref,
                      *, lam_init):
    t = ATTN_T
    qi = pl.program_id(2)
    acc_ref[...] = jnp.zeros(acc_ref.shape, F32)
    key_le_query = (lax.broadcasted_iota(jnp.int32, (t, t), 0)
                    <= lax.broadcasted_iota(jnp.int32, (t, t), 1))

    def block(ki, stats, masked):
        rows = pl.ds(pl.multiple_of(ki * t, t), t)
        vt = vt_ref[ki]
        new_stats = []
        for c, (q_ref, k_ref) in enumerate(((q0_ref, k0_ref), (q1_ref, k1_ref))):
            m_old, l_old = stats[c]
            st = lax.dot_general(k_ref[rows, :], q_ref[...], NT_DIMS,
                                 preferred_element_type=F32)
            if masked:
                st = jnp.where(key_le_query, st, -jnp.inf)
            m_new = jnp.maximum(m_old, jnp.max(st, axis=0, keepdims=True))
            alpha = jnp.exp2(m_old - m_new)
            pt = jnp.exp2(st - m_new)
            l_new = alpha * l_old + jnp.sum(pt, axis=0, keepdims=True)
            acc_ref[c] = alpha * acc_ref[c] + jnp.dot(vt, pt.astype(BF16),
                                                      preferred_element_type=F32)
            new_stats.append((m_new, l_new))
        return tuple(new_stats)

    init = ((jnp.full((1, t), -jnp.inf, F32), jnp.zeros((1, t), F32)),) * 2
    stats = lax.fori_loop(0, qi, lambda ki, s: block(ki, s, False), init)
    (_, l0), (_, l1) = block(qi, stats, True)

    lp = lam_ref[...]
    lam = (jnp.exp(jnp.sum(lp[0:1] * lp[1:2], axis=-1, keepdims=True))
           - jnp.exp(jnp.sum(lp[2:3] * lp[3:4], axis=-1, keepdims=True)) + lam_init)
    ot = acc_ref[0] / l0 - lam * (acc_ref[1] / l1)
    ot = ot * lax.rsqrt(jnp.mean(ot * ot, axis=0, keepdims=True) + EPS)
    o_ref[...] = (ot.T * gs_ref[...] * (1.0 - lam_init)).astype(BF16)


def _diff_attn(q, k, vt, b_lambda, subln_g, j, lam_init, batch, seq):
    n, qk_w = q.shape
    heads = qk_w // (2 * HEAD_DIM)
    _, nk, _, vd, t = vt.shape
    q3 = q.reshape(batch, seq, qk_w)
    k3 = k.reshape(batch, seq, qk_w)
    o = pl.pallas_call(
        functools.partial(_diff_attn_kernel, lam_init=lam_init),
        grid=(batch, heads, seq // t),
        in_specs=[
            pl.BlockSpec((None, 4, HEAD_DIM), lambda b, h, i: (j, 0, 0)),
            pl.BlockSpec((None, 1, vd), lambda b, h, i: (j, 0, 0)),
            pl.BlockSpec((None, t, HEAD_DIM), lambda b, h, i: (b, i, h)),
            pl.BlockSpec((None, t, HEAD_DIM), lambda b, h, i: (b, i, heads + h)),
            pl.BlockSpec((None, seq, HEAD_DIM), lambda b, h, i: (b, 0, h)),
            pl.BlockSpec((None, seq, HEAD_DIM), lambda b, h, i: (b, 0, heads + h)),
            pl.BlockSpec((None, nk, None, vd, t), lambda b, h, i: (b, 0, h, 0, 0)),
        ],
        out_specs=pl.BlockSpec((None, t, vd), lambda b, h, i: (b, i, h)),
        out_shape=jax.ShapeDtypeStruct((batch, seq, heads * vd), BF16),
        scratch_shapes=[pltpu.VMEM((2, vd, t), F32)],
        compiler_params=_params("parallel", "parallel", "parallel"),
        name="diff_attn",
    )(b_lambda, subln_g, q3, q3, k3, k3, vt)
    return o.reshape(n, heads * vd)


def _out_proj_kernel(x_ref, mix_ref, mo_ref, w_ref, o_ref):
    mix_w = mix_ref.shape[1]
    tn = PROJ_TN
    for c in range(o_ref.shape[1] // tn):
        cols = slice(c * tn, (c + 1) * tn)
        y = (jnp.dot(mix_ref[...], w_ref[:mix_w, cols], preferred_element_type=F32)
             + jnp.dot(mo_ref[...], w_ref[mix_w:, cols], preferred_element_type=F32))
        o_ref[:, cols] = x_ref[:, cols] + y


def _out_proj(x, mix, mo, w_out, l):
    n, d = x.shape
    mix_w = mix.shape[1]
    tm = OUT_TM
    return pl.pallas_call(
        _out_proj_kernel,
        grid=(n // tm,),
        in_specs=[
            pl.BlockSpec((tm, d), lambda r: (r, 0)),
            pl.BlockSpec((tm, mix_w), lambda r: (r, 0)),
            pl.BlockSpec((tm, MEM_W), lambda r: (r, 0)),
            pl.BlockSpec((None, d, d), lambda r: (l, 0, 0)),
        ],
        out_specs=pl.BlockSpec((tm, d), lambda r: (r, 0)),
        out_shape=jax.ShapeDtypeStruct((n, d), F32),
        compiler_params=_params("parallel"),
        name="out_proj",
    )(x, mix, mo, w_out)


def kernel(x, mem, norm_g, ffn_w_gu, ffn_w_down, w_out, mem_norm_g, mem_w_kv, mem_q_norm_g,
           mem_k_norm_g, a_w_in, a_v_norm_g, a_w_s, a_b_s, kv_norm_g, w_kv, k_norm_g, b_w_in,
           b_q_norm_g, b_lambda, b_subln_g):
    batch, seq, d = x.shape
    depth = norm_g.shape[0]
    n_a = a_w_in.shape[0]
    mix_w = d - MEM_W
    n = batch * seq

    ffn_w_gu, ffn_w_down, w_out, mem_w_kv, a_w_in, w_kv, b_w_in = (
        w.astype(BF16) for w in (ffn_w_gu, ffn_w_down, w_out, mem_w_kv, a_w_in, w_kv, b_w_in))

    norm_g = norm_g[:, :, None, :]
    mem_norm_g = mem_norm_g[:, None, :]
    mem_q_norm_g = mem_q_norm_g[:, None, :]
    mem_k_norm_g = mem_k_norm_g[:, None, :]
    a_v_norm_g = a_v_norm_g[:, None, :]
    b_q_norm_g = b_q_norm_g[:, None, :]
    b_subln_g = b_subln_g[:, None, :]
    a_b_s_t = jnp.swapaxes(a_b_s, 1, 2)

    pos = jnp.arange(seq, dtype=F32)
    inv = ROPE_THETA ** (-jnp.arange(0, HEAD_DIM, 2, dtype=F32) / HEAD_DIM)
    ang = pos[:, None] * inv[None, :]
    ang = jnp.concatenate([ang, ang], axis=-1)
    cos = jnp.cos(ang)
    sign = jnp.where(jnp.arange(HEAD_DIM) < HEAD_DIM // 2, -1.0, 1.0).astype(F32)
    sin_signed = jnp.sin(ang) * sign[None, :]

    x = x.reshape(n, d)
    mk, mv = _mem_kv(mem.reshape(batch * mem.shape[1], d), mem_norm_g, mem_w_kv, mem_k_norm_g)

    k_sh = v_sh = None
    for l in range(depth):
        if l == n_a:
            k_sh, v_sh = _kv_proj(x, kv_norm_g[None, :], w_kv, k_norm_g[None, :], cos, sin_signed,
                                  batch, seq, mix_w, b_subln_g.shape[-1])
        x = _ffn(x, norm_g, ffn_w_gu, ffn_w_down, l, 0, 0)
        if l < n_a:
            mix, mo = _mixer_a(x, norm_g, l, a_w_in, a_v_norm_g, a_w_s, a_b_s_t, mem_q_norm_g,
                               mk, mv, seq)
        else:
            j = l - n_a
            q, mo = _proj_b(x, norm_g, l, b_w_in, j, b_q_norm_g, cos, sin_signed, mem_q_norm_g,
                            mk, mv, seq)
            lam_init = 0.8 - 0.6 * math.exp(-0.3 * l)
            mix = _diff_attn(q, k_sh, v_sh, b_lambda, b_subln_g, j, lam_init, batch, seq)
        x = _out_proj(x, mix, mo, w_out, l)
        x = _ffn(x, norm_g, ffn_w_gu, ffn_w_down, l, 2, 1)
    return x.reshape(batch, seq, d)
```

```python
import functools
import math

import jax
import jax.numpy as jnp
from jax import lax
from jax.experimental import pallas as pl
from jax.experimental.pallas import tpu as pltpu

EPS = 1e-6
HEAD_DIM = 128
MEM_HEADS = 4
MEM_W = MEM_HEADS * HEAD_DIM
CHUNK = 128
GMLP_GROUPS = 6
ROPE_THETA = 10000.0
ATTN_SCALE = HEAD_DIM ** -0.5
SQRT_HALF = 0.7071067811865476

V7X_VMEM_BYTES = 64 * 1024 * 1024
VMEM_LIMIT_BYTES = V7X_VMEM_BYTES - 8 * 1024 * 1024

FFN_TM = 1024
FFN_TF = 512
PROJ_TM = 256
PROJ_TN = 512
OUT_TM = 512
ATTN_T = 512
LOG2E = 1.4426950408889634
Q_PRESCALE = ATTN_SCALE * LOG2E

F32 = jnp.float32
BF16 = jnp.bfloat16
NT_DIMS = (((1,), (1,)), ((), ()))


def _params(*semantics):
    return pltpu.CompilerParams(dimension_semantics=semantics,
                                vmem_limit_bytes=VMEM_LIMIT_BYTES)


def _rms(x, g):
    return x * lax.rsqrt(jnp.mean(x * x, axis=-1, keepdims=True) + EPS) * g


def _gelu(z):
    return 0.5 * z * (1.0 + lax.erf(z * SQRT_HALF))


def _rope(r, cos, sin_signed):
    return r * cos + pltpu.roll(r, HEAD_DIM // 2, 1) * sin_signed


def _mem_attn(zq, gq, mk_ref, mv_ref, mo_ref):
    for hh in range(MEM_HEADS):
        cols = slice(hh * HEAD_DIM, (hh + 1) * HEAD_DIM)
        qh = _rms(zq[:, cols], gq).astype(BF16)
        s = lax.dot_general(qh, mk_ref[:, cols], NT_DIMS, preferred_element_type=F32) * ATTN_SCALE
        e = jnp.exp(s - jnp.max(s, axis=-1, keepdims=True))
        p = e / jnp.sum(e, axis=-1, keepdims=True)
        mo_ref[:, cols] = jnp.dot(p.astype(BF16), mv_ref[:, cols],
                                  preferred_element_type=F32).astype(BF16)


def _ffn_kernel(x_ref, g_ref, wg_ref, wu_ref, wd_ref, o_ref, h_ref):
    @pl.when(pl.program_id(1) == 0)
    def _():
        x = x_ref[...]
        h_ref[...] = _rms(x, g_ref[...]).astype(BF16)
        o_ref[...] = x

    h = h_ref[...]
    g = jnp.dot(h, wg_ref[...], preferred_element_type=F32)
    u = jnp.dot(h, wu_ref[...], preferred_element_type=F32)
    act = (0.5 * g * jax.nn.sigmoid(g)) * u
    o_ref[...] += jnp.dot(act.astype(BF16), wd_ref[...], preferred_element_type=F32)


def _ffn(x, norm_g, w_gu, w_down, l, slot, i):
    n, d = x.shape
    f = w_down.shape[2]
    tm, tf = FFN_TM, FFN_TF
    nf = f // tf
    return pl.pallas_call(
        _ffn_kernel,
        grid=(n // tm, nf),
        in_specs=[
            pl.BlockSpec((tm, d), lambda r, j: (r, 0)),
            pl.BlockSpec((None, None, 1, d), lambda r, j: (l, slot, 0, 0)),
            pl.BlockSpec((None, None, d, tf), lambda r, j: (l, i, 0, j)),
            pl.BlockSpec((None, None, d, tf), lambda r, j: (l, i, 0, j + nf)),
            pl.BlockSpec((None, None, tf, d), lambda r, j: (l, i, j, 0)),
        ],
        out_specs=pl.BlockSpec((tm, d), lambda r, j: (r, 0)),
        out_shape=jax.ShapeDtypeStruct((n, d), F32),
        scratch_shapes=[pltpu.VMEM((tm, d), BF16)],
        compiler_params=_params("parallel", "arbitrary"),
        name="ffn",
    )(x, norm_g, w_gu, w_gu, w_down)


def _mem_kv_kernel(mem_ref, g_ref, w_ref, gk_ref, mk_ref, mv_ref):
    h = _rms(mem_ref[...], g_ref[...]).astype(BF16)
    kv = jnp.dot(h, w_ref[...], preferred_element_type=F32)
    gk = gk_ref[...]
    for hh in range(MEM_HEADS):
        cols = slice(hh * HEAD_DIM, (hh + 1) * HEAD_DIM)
        mk_ref[:, cols] = _rms(kv[:, cols], gk).astype(BF16)
    mv_ref[...] = kv[:, MEM_W:].astype(BF16)


def _mem_kv(mem2, mem_norm_g, mem_w_kv, mem_k_norm_g):
    rows, d = mem2.shape
    depth = mem_w_kv.shape[0]
    out = jax.ShapeDtypeStruct((depth, rows, MEM_W), BF16)
    return pl.pallas_call(
        _mem_kv_kernel,
        grid=(depth,),
        in_specs=[
            pl.BlockSpec((rows, d), lambda l: (0, 0)),
            pl.BlockSpec((None, 1, d), lambda l: (l, 0, 0)),
            pl.BlockSpec((None, d, 2 * MEM_W), lambda l: (l, 0, 0)),
            pl.BlockSpec((None, 1, HEAD_DIM), lambda l: (l, 0, 0)),
        ],
        out_specs=[pl.BlockSpec((None, rows, MEM_W), lambda l: (l, 0, 0))] * 2,
        out_shape=[out, out],
        compiler_params=_params("parallel"),
        name="mem_kv",
    )(mem2, mem_norm_g, mem_w_kv, mem_k_norm_g)


def _mixer_a_kernel(x_ref, g_ref, w_ref, gv_ref, ws_ref, bs_ref, gq_ref, mk_ref, mv_ref,
                    mix_ref, mo_ref, h_ref, u_ref, v_ref):
    tm = x_ref.shape[0]
    mix_w = u_ref.shape[1]
    tn = PROJ_TN
    h_ref[...] = _rms(x_ref[...], g_ref[...]).astype(BF16)
    for c in range(mix_w // tn):
        cols = slice(c * tn, (c + 1) * tn)
        u_ref[:, cols] = _gelu(jnp.dot(h_ref[...], w_ref[:, cols], preferred_element_type=F32))
        v_ref[:, cols] = _gelu(jnp.dot(h_ref[...], w_ref[:, mix_w + c * tn:mix_w + (c + 1) * tn],
                                       preferred_element_type=F32))
    zq = jnp.dot(h_ref[...], w_ref[:, 2 * mix_w:], preferred_element_type=F32)
    _mem_attn(zq, gq_ref[...], mk_ref, mv_ref, mo_ref)

    gw = mix_w // GMLP_GROUPS
    causal = (lax.broadcasted_iota(jnp.int32, (CHUNK, CHUNK), 0)
              >= lax.broadcasted_iota(jnp.int32, (CHUNK, CHUNK), 1))
    w_s = [jnp.where(causal, ws_ref[grp], 0.0).astype(BF16) for grp in range(GMLP_GROUPS)]
    for t in range(tm // CHUNK):
        rows = slice(t * CHUNK, (t + 1) * CHUNK)
        vn = _rms(v_ref[rows, :], gv_ref[...]).astype(BF16)
        for grp in range(GMLP_GROUPS):
            cols = slice(grp * gw, (grp + 1) * gw)
            mixed = (jnp.dot(w_s[grp], vn[:, cols], preferred_element_type=F32)
                     + bs_ref[:, grp:grp + 1])
            mix_ref[rows, cols] = (u_ref[rows, cols] * mixed).astype(BF16)


def _mixer_a(x, norm_g, l, w_in, v_norm_g, w_s, b_s_t, mem_q_norm_g, mk, mv, seq):
    n, d = x.shape
    mix_w = v_norm_g.shape[-1]
    tm = PROJ_TM
    blocks_per_seq = seq // tm
    mem_len = mk.shape[1] // (n // seq)
    mem_spec = pl.BlockSpec((None, mem_len, MEM_W), lambda r: (l, r // blocks_per_seq, 0))
    return pl.pallas_call(
        _mixer_a_kernel,
        grid=(n // tm,),
        in_specs=[
            pl.BlockSpec((tm, d), lambda r: (r, 0)),
            pl.BlockSpec((None, None, 1, d), lambda r: (l, 1, 0, 0)),
            pl.BlockSpec((None, d, 2 * mix_w + MEM_W), lambda r: (l, 0, 0)),
            pl.BlockSpec((None, 1, mix_w), lambda r: (l, 0, 0)),
            pl.BlockSpec((None, GMLP_GROUPS, CHUNK, CHUNK), lambda r: (l, 0, 0, 0)),
            pl.BlockSpec((None, CHUNK, GMLP_GROUPS), lambda r: (l, 0, 0)),
            pl.BlockSpec((None, 1, HEAD_DIM), lambda r: (l, 0, 0)),
            mem_spec, mem_spec,
        ],
        out_specs=[pl.BlockSpec((tm, mix_w), lambda r: (r, 0)),
                   pl.BlockSpec((tm, MEM_W), lambda r: (r, 0))],
        out_shape=[jax.ShapeDtypeStruct((n, mix_w), BF16),
                   jax.ShapeDtypeStruct((n, MEM_W), BF16)],
        scratch_shapes=[pltpu.VMEM((tm, d), BF16),
                        pltpu.VMEM((tm, mix_w), F32),
                        pltpu.VMEM((tm, mix_w), F32)],
        compiler_params=_params("parallel"),
        name="mixer_a",
    )(x, norm_g, w_in, v_norm_g, w_s, b_s_t, mem_q_norm_g, mk, mv)


def _kv_proj_kernel(x_ref, g_ref, w_ref, gk_ref, cos_ref, sin_ref, k_ref, vt_ref, h_ref, v_ref):
    qk_w = k_ref.shape[1]
    heads, vd, _ = vt_ref.shape
    tn = PROJ_TN
    h_ref[...] = _rms(x_ref[...], g_ref[...]).astype(BF16)
    gk, cos, sin = gk_ref[...], cos_ref[...], sin_ref[...]
    for c in range(qk_w // tn):
        z = jnp.dot(h_ref[...], w_ref[:, c * tn:(c + 1) * tn], preferred_element_type=F32)
        for k in range(tn // HEAD_DIM):
            r = _rms(z[:, k * HEAD_DIM:(k + 1) * HEAD_DIM], gk)
            k_ref[:, c * tn + k * HEAD_DIM:c * tn + (k + 1) * HEAD_DIM] = _rope(r, cos, sin).astype(BF16)
    for hh in range(heads):
        v_ref[...] = jnp.dot(h_ref[...], w_ref[:, qk_w + hh * vd:qk_w + (hh + 1) * vd],
                             preferred_element_type=F32)
        vt_ref[hh] = v_ref[...].T.astype(BF16)


def _kv_proj(x, kv_norm_g, w_kv, k_norm_g, cos, sin_signed, batch, seq, qk_w, vd):
    n, d = x.shape
    heads = (w_kv.shape[1] - qk_w) // vd
    tm = ATTN_T
    blocks_per_seq = seq // tm
    rope_spec = pl.BlockSpec((tm, HEAD_DIM), lambda r: (r % blocks_per_seq, 0))
    return pl.pallas_call(
        _kv_proj_kernel,
        grid=(n // tm,),
        in_specs=[
            pl.BlockSpec((tm, d), lambda r: (r, 0)),
            pl.BlockSpec((1, d), lambda r: (0, 0)),
            pl.BlockSpec((d, qk_w + heads * vd), lambda r: (0, 0)),
            pl.BlockSpec((1, HEAD_DIM), lambda r: (0, 0)),
            rope_spec, rope_spec,
        ],
        out_specs=[pl.BlockSpec((tm, qk_w), lambda r: (r, 0)),
                   pl.BlockSpec((None, None, heads, vd, tm),
                                lambda r: (r // blocks_per_seq, r % blocks_per_seq, 0, 0, 0))],
        out_shape=[jax.ShapeDtypeStruct((n, qk_w), BF16),
                   jax.ShapeDtypeStruct((batch, blocks_per_seq, heads, vd, tm), BF16)],
        scratch_shapes=[pltpu.VMEM((tm, d), BF16), pltpu.VMEM((tm, vd), F32)],
        compiler_params=_params("parallel"),
        name="kv_proj",
    )(x, kv_norm_g, w_kv, k_norm_g, cos, sin_signed)


def _proj_b_kernel(x_ref, g_ref, w_ref, gqn_ref, cos_ref, sin_ref, gq_ref, mk_ref, mv_ref,
                   q_ref, mo_ref, h_ref):
    qk_w = q_ref.shape[1]
    tn = PROJ_TN
    h_ref[...] = _rms(x_ref[...], g_ref[...]).astype(BF16)
    gqn, cos, sin = gqn_ref[...], cos_ref[...], sin_ref[...]
    for c in range(qk_w // tn):
        z = jnp.dot(h_ref[...], w_ref[:, c * tn:(c + 1) * tn], preferred_element_type=F32)
        for k in range(tn // HEAD_DIM):
            r = _rms(z[:, k * HEAD_DIM:(k + 1) * HEAD_DIM], gqn)
            q_ref[:, c * tn + k * HEAD_DIM:c * tn + (k + 1) * HEAD_DIM] = (
                _rope(r, cos, sin) * Q_PRESCALE).astype(BF16)
    zq = jnp.dot(h_ref[...], w_ref[:, qk_w:], preferred_element_type=F32)
    _mem_attn(zq, gq_ref[...], mk_ref, mv_ref, mo_ref)


def _proj_b(x, norm_g, l, w_in, j, q_norm_g, cos, sin_signed, mem_q_norm_g, mk, mv, seq):
    n, d = x.shape
    qk_w = w_in.shape[2] - MEM_W
    tm = PROJ_TM
    blocks_per_seq = seq // tm
    mem_len = mk.shape[1] // (n // seq)
    rope_spec = pl.BlockSpec((tm, HEAD_DIM), lambda r: (r % blocks_per_seq, 0))
    mem_spec = pl.BlockSpec((None, mem_len, MEM_W), lambda r: (l, r // blocks_per_seq, 0))
    return pl.pallas_call(
        _proj_b_kernel,
        grid=(n // tm,),
        in_specs=[
            pl.BlockSpec((tm, d), lambda r: (r, 0)),
            pl.BlockSpec((None, None, 1, d), lambda r: (l, 1, 0, 0)),
            pl.BlockSpec((None, d, qk_w + MEM_W), lambda r: (j, 0, 0)),
            pl.BlockSpec((None, 1, HEAD_DIM), lambda r: (j, 0, 0)),
            rope_spec, rope_spec,
            pl.BlockSpec((None, 1, HEAD_DIM), lambda r: (l, 0, 0)),
            mem_spec, mem_spec,
        ],
        out_specs=[pl.BlockSpec((tm, qk_w), lambda r: (r, 0)),
                   pl.BlockSpec((tm, MEM_W), lambda r: (r, 0))],
        out_shape=[jax.ShapeDtypeStruct((n, qk_w), BF16),
                   jax.ShapeDtypeStruct((n, MEM_W), BF16)],
        scratch_shapes=[pltpu.VMEM((tm, d), BF16)],
        compiler_params=_params("parallel"),
        name="proj_b",
    )(x, norm_g, w_in, q_norm_g, cos, sin_signed, mem_q_norm_g, mk, mv)


def _diff_attn_kernel(lam_ref, gs_ref, q0_ref, q1_ref, k0_ref, k1_ref, vt_ref, o_ref, acc_ref,
                      *, lam_init):
    t = ATTN_T
    qi = pl.program_id(2)
    q_refs, k_refs = (q0_ref, q1_ref), (k0_ref, k1_ref)
    acc_ref[...] = jnp.zeros(acc_ref.shape, F32)

    def scores(ki, c):
        rows = pl.ds(pl.multiple_of(ki * t, t), t)
        return lax.dot_general(k_refs[c][rows, :], q_refs[c][...], NT_DIMS,
                               preferred_element_type=F32)

    def update(ki, c, st, stat, masked):
        m_old, l_old = stat
        if masked:
            key_le_query = (lax.broadcasted_iota(jnp.int32, (t, t), 0)
                            <= lax.broadcasted_iota(jnp.int32, (t, t), 1))
            st = jnp.where(key_le_query, st, -jnp.inf)
        m_new = jnp.maximum(m_old, jnp.max(st, axis=0, keepdims=True))
        alpha = jnp.exp2(m_old - m_new)
        pt = jnp.exp2(st - m_new)
        acc_ref[c] = alpha * acc_ref[c] + jnp.dot(vt_ref[ki], pt.astype(BF16),
                                                  preferred_element_type=F32)
        return m_new, alpha * l_old + jnp.sum(pt, axis=0, keepdims=True)

    def blocks(kis, stats, masked_last):
        stats = list(stats)
        work = [(ki, c, masked_last and n == len(kis) - 1)
                for n, ki in enumerate(kis) for c in range(2)]
        pending = [scores(ki, c) for ki, c, _ in work[:2]]
        for n, (ki, c, masked) in enumerate(work):
            if n + 2 < len(work):
                pending.append(scores(work[n + 2][0], work[n + 2][1]))
            stats[c] = update(ki, c, pending[n], stats[c], masked)
        return tuple(stats)

    init = ((jnp.full((1, t), -jnp.inf, F32), jnp.zeros((1, t), F32)),) * 2
    stats = lax.fori_loop(0, qi // 2, lambda j, s: blocks((2 * j, 2 * j + 1), s, False), init)
    stats = lax.cond(qi % 2 == 1,
                     lambda s: blocks((qi - 1, qi), s, True),
                     lambda s: blocks((qi,), s, True), stats)
    l0, l1 = stats[0][1], stats[1][1]

    lp = lam_ref[...]
    lam = (jnp.exp(jnp.sum(lp[0:1] * lp[1:2], axis=-1, keepdims=True))
           - jnp.exp(jnp.sum(lp[2:3] * lp[3:4], axis=-1, keepdims=True)) + lam_init)
    ot = acc_ref[0] / l0 - lam * (acc_ref[1] / l1)
    ot = ot * lax.rsqrt(jnp.mean(ot * ot, axis=0, keepdims=True) + EPS)
    o_ref[...] = (ot.T * gs_ref[...] * (1.0 - lam_init)).astype(BF16)


def _diff_attn(q, k, vt, b_lambda, subln_g, j, lam_init, batch, seq):
    n, qk_w = q.shape
    heads = qk_w // (2 * HEAD_DIM)
    _, nk, _, vd, t = vt.shape
    q3 = q.reshape(batch, seq, qk_w)
    k3 = k.reshape(batch, seq, qk_w)
    o = pl.pallas_call(
        functools.partial(_diff_attn_kernel, lam_init=lam_init),
        grid=(batch, heads, seq // t),
        in_specs=[
            pl.BlockSpec((None, 4, HEAD_DIM), lambda b, h, i: (j, 0, 0)),
            pl.BlockSpec((None, 1, vd), lambda b, h, i: (j, 0, 0)),
            pl.BlockSpec((None, t, HEAD_DIM), lambda b, h, i: (b, i, h)),
            pl.BlockSpec((None, t, HEAD_DIM), lambda b, h, i: (b, i, heads + h)),
            pl.BlockSpec((None, seq, HEAD_DIM), lambda b, h, i: (b, 0, h)),
            pl.BlockSpec((None, seq, HEAD_DIM), lambda b, h, i: (b, 0, heads + h)),
            pl.BlockSpec((None, nk, None, vd, t), lambda b, h, i: (b, 0, h, 0, 0)),
        ],
        out_specs=pl.BlockSpec((None, t, vd), lambda b, h, i: (b, i, h)),
        out_shape=jax.ShapeDtypeStruct((batch, seq, heads * vd), BF16),
        scratch_shapes=[pltpu.VMEM((2, vd, t), F32)],
        compiler_params=_params("parallel", "parallel", "parallel"),
        name="diff_attn",
    )(b_lambda, subln_g, q3, q3, k3, k3, vt)
    return o.reshape(n, heads * vd)


def _out_proj_kernel(x_ref, mix_ref, mo_ref, w_ref, o_ref):
    mix_w = mix_ref.shape[1]
    tn = PROJ_TN
    for c in range(o_ref.shape[1] // tn):
        cols = slice(c * tn, (c + 1) * tn)
        y = (jnp.dot(mix_ref[...], w_ref[:mix_w, cols], preferred_element_type=F32)
             + jnp.dot(mo_ref[...], w_ref[mix_w:, cols], preferred_element_type=F32))
        o_ref[:, cols] = x_ref[:, cols] + y


def _out_proj(x, mix, mo, w_out, l):
    n, d = x.shape
    mix_w = mix.shape[1]
    tm = OUT_TM
    return pl.pallas_call(
        _out_proj_kernel,
        grid=(n // tm,),
        in_specs=[
            pl.BlockSpec((tm, d), lambda r: (r, 0)),
            pl.BlockSpec((tm, mix_w), lambda r: (r, 0)),
            pl.BlockSpec((tm, MEM_W), lambda r: (r, 0)),
            pl.BlockSpec((None, d, d), lambda r: (l, 0, 0)),
        ],
        out_specs=pl.BlockSpec((tm, d), lambda r: (r, 0)),
        out_shape=jax.ShapeDtypeStruct((n, d), F32),
        compiler_params=_params("parallel"),
        name="out_proj",
    )(x, mix, mo, w_out)


def kernel(x, mem, norm_g, ffn_w_gu, ffn_w_down, w_out, mem_norm_g, mem_w_kv, mem_q_norm_g,
           mem_k_norm_g, a_w_in, a_v_norm_g, a_w_s, a_b_s, kv_norm_g, w_kv, k_norm_g, b_w_in,
           b_q_norm_g, b_lambda, b_subln_g):
    batch, seq, d = x.shape
    depth = norm_g.shape[0]
    n_a = a_w_in.shape[0]
    mix_w = d - MEM_W
    n = batch * seq

    ffn_w_gu, ffn_w_down, w_out, mem_w_kv, a_w_in, w_kv, b_w_in = (
        w.astype(BF16) for w in (ffn_w_gu, ffn_w_down, w_out, mem_w_kv, a_w_in, w_kv, b_w_in))

    norm_g = norm_g[:, :, None, :]
    mem_norm_g = mem_norm_g[:, None, :]
    mem_q_norm_g = mem_q_norm_g[:, None, :]
    mem_k_norm_g = mem_k_norm_g[:, None, :]
    a_v_norm_g = a_v_norm_g[:, None, :]
    b_q_norm_g = b_q_norm_g[:, None, :]
    b_subln_g = b_subln_g[:, None, :]
    a_b_s_t = jnp.swapaxes(a_b_s, 1, 2)

    pos = jnp.arange(seq, dtype=F32)
    inv = ROPE_THETA ** (-jnp.arange(0, HEAD_DIM, 2, dtype=F32) / HEAD_DIM)
    ang = pos[:, None] * inv[None, :]
    ang = jnp.concatenate([ang, ang], axis=-1)
    cos = jnp.cos(ang)
    sign = jnp.where(jnp.arange(HEAD_DIM) < HEAD_DIM // 2, -1.0, 1.0).astype(F32)
    sin_signed = jnp.sin(ang) * sign[None, :]

    x = x.reshape(n, d)
    mk, mv = _mem_kv(mem.reshape(batch * mem.shape[1], d), mem_norm_g, mem_w_kv, mem_k_norm_g)

    k_sh = v_sh = None
    for l in range(depth):
        if l == n_a:
            k_sh, v_sh = _kv_proj(x, kv_norm_g[None, :], w_kv, k_norm_g[None, :], cos, sin_signed,
                                  batch, seq, mix_w, b_subln_g.shape[-1])
        x = _ffn(x, norm_g, ffn_w_gu, ffn_w_down, l, 0, 0)
        if l < n_a:
            mix, mo = _mixer_a(x, norm_g, l, a_w_in, a_v_norm_g, a_w_s, a_b_s_t, mem_q_norm_g,
                               mk, mv, seq)
        else:
            j = l - n_a
            q, mo = _proj_b(x, norm_g, l, b_w_in, j, b_q_norm_g, cos, sin_signed, mem_q_norm_g,
                            mk, mv, seq)
            lam_init = 0.8 - 0.6 * math.exp(-0.3 * l)
            mix = _diff_attn(q, k_sh, v_sh, b_lambda, b_subln_g, j, lam_init, batch, seq)
        x = _out_proj(x, mix, mo, w_out, l)
        x = _ffn(x, norm_g, ffn_w_gu, ffn_w_down, l, 2, 1)
    return x.reshape(batch, seq, d)
```

```python
import functools
import math

import jax
import jax.numpy as jnp
from jax import lax
from jax.experimental import pallas as pl
from jax.experimental.pallas import tpu as pltpu

EPS = 1e-6
HEAD_DIM = 128
MEM_HEADS = 4
MEM_W = MEM_HEADS * HEAD_DIM
CHUNK = 128
GMLP_GROUPS = 6
ROPE_THETA = 10000.0
ATTN_SCALE = HEAD_DIM ** -0.5
SQRT_HALF = 0.7071067811865476

V7X_VMEM_BYTES = 64 * 1024 * 1024
VMEM_LIMIT_BYTES = V7X_VMEM_BYTES - 8 * 1024 * 1024
FFN_VMEM_LIMIT_BYTES = V7X_VMEM_BYTES - 5 * 1024 * 1024
FFN_TM = 1024
FFN_TF = 512
PROJ_TM = 512
PROJ_SUB = 256
PROJ_TN = 512
OUT_TM = 512
ATTN_T = 512
LOG2E = 1.4426950408889634
Q_PRESCALE = ATTN_SCALE * LOG2E

F32 = jnp.float32
BF16 = jnp.bfloat16
NT_DIMS = (((1,), (1,)), ((), ()))


def _params(*semantics):
    return pltpu.CompilerParams(dimension_semantics=semantics,
                                vmem_limit_bytes=VMEM_LIMIT_BYTES)


def _rms(x, g):
    return x * lax.rsqrt(jnp.mean(x * x, axis=-1, keepdims=True) + EPS) * g


def _gelu(z):
    return 0.5 * z * (1.0 + lax.erf(z * SQRT_HALF))


def _rope(r, cos, sin_signed):
    return r * cos + pltpu.roll(r, HEAD_DIM // 2, 1) * sin_signed


def _mem_attn(zq, gq, mk_ref, mv_ref, mo_ref, rows):
    for hh in range(MEM_HEADS):
        cols = slice(hh * HEAD_DIM, (hh + 1) * HEAD_DIM)
        qh = _rms(zq[:, cols], gq).astype(BF16)
        s = lax.dot_general(qh, mk_ref[:, cols], NT_DIMS, preferred_element_type=F32) * ATTN_SCALE
        e = jnp.exp(s - jnp.max(s, axis=-1, keepdims=True))
        p = e / jnp.sum(e, axis=-1, keepdims=True)
        mo_ref[rows, cols] = jnp.dot(p.astype(BF16), mv_ref[:, cols],
                                     preferred_element_type=F32).astype(BF16)


def _sub_blocks(tm):
    return [slice(s, s + PROJ_SUB) for s in range(0, tm, PROJ_SUB)]


def _ffn_kernel(x_ref, g_ref, wg_ref, wu_ref, wd_ref, *refs):
    n_cast = (len(refs) - 2) // 2
    cast_in, o_ref, cast_out, h_ref = refs[:n_cast], refs[n_cast], refs[n_cast + 1:-1], refs[-1]
    j = pl.program_id(1)

    def step(first):
        if first:
            h_ref[...] = _rms(x_ref[...], g_ref[...]).astype(BF16)
        h = h_ref[...]
        g = jnp.dot(h, wg_ref[...], preferred_element_type=F32)
        u = jnp.dot(h, wu_ref[...], preferred_element_type=F32)
        for src_ref, dst_ref in zip(cast_in, cast_out):
            dst_ref[...] = src_ref[...].astype(BF16)
        act = (0.5 * g * jax.nn.sigmoid(g)) * u
        down = jnp.dot(act.astype(BF16), wd_ref[...], preferred_element_type=F32)
        if first:
            o_ref[...] = x_ref[...] + down
        else:
            o_ref[...] += down

    pl.when(j == 0)(functools.partial(step, True))
    pl.when(j > 0)(functools.partial(step, False))


def _ffn(x, norm_g, l, slot, w_gu, w_down, next_w=None):
    n, d = x.shape
    f = w_down.shape[0]
    tm, tf = FFN_TM, FFN_TF
    nr, nf = n // tm, f // tf
    in_specs = [
        pl.BlockSpec((tm, d), lambda r, j: (r, 0)),
        pl.BlockSpec((None, None, 1, d), lambda r, j: (l, slot, 0, 0)),
        pl.BlockSpec((d, tf), lambda r, j: (0, j)),
        pl.BlockSpec((d, tf), lambda r, j: (0, j + nf)),
        pl.BlockSpec((tf, d), lambda r, j: (j, 0)),
    ]
    out_specs = [pl.BlockSpec((tm, d), lambda r, j: (r, 0))]
    out_shape = [jax.ShapeDtypeStruct((n, d), F32)]
    args = [x, norm_g, w_gu, w_gu, w_down]
    if next_w is not None:
        gu32, down32, l2, i2 = next_w
        assert d % nr == 0 and (2 * f) % nf == 0 and f % nf == 0
        in_specs += [
            pl.BlockSpec((None, None, d // nr, 2 * f // nf), lambda r, j: (l2, i2, r, j)),
            pl.BlockSpec((None, None, f // nf, d // nr), lambda r, j: (l2, i2, j, r)),
        ]
        out_specs += [pl.BlockSpec((d // nr, 2 * f // nf), lambda r, j: (r, j)),
                      pl.BlockSpec((f // nf, d // nr), lambda r, j: (j, r))]
        out_shape += [jax.ShapeDtypeStruct((d, 2 * f), BF16), jax.ShapeDtypeStruct((f, d), BF16)]
        args += [gu32, down32]
    return pl.pallas_call(
        _ffn_kernel,
        grid=(nr, nf),
        in_specs=in_specs,
        out_specs=out_specs,
        out_shape=out_shape,
        scratch_shapes=[pltpu.VMEM((tm, d), BF16)],
        compiler_params=pltpu.CompilerParams(dimension_semantics=("parallel", "arbitrary"),
                                             vmem_limit_bytes=FFN_VMEM_LIMIT_BYTES),
        name="ffn",
    )(*args)


def _mem_kv_kernel(mem_ref, g_ref, w_ref, gk_ref, mk_ref, mv_ref):
    h = _rms(mem_ref[...], g_ref[...]).astype(BF16)
    kv = jnp.dot(h, w_ref[...], preferred_element_type=F32)
    gk = gk_ref[...]
    for hh in range(MEM_HEADS):
        cols = slice(hh * HEAD_DIM, (hh + 1) * HEAD_DIM)
        mk_ref[:, cols] = _rms(kv[:, cols], gk).astype(BF16)
    mv_ref[...] = kv[:, MEM_W:].astype(BF16)


def _mem_kv(mem2, mem_norm_g, mem_w_kv, mem_k_norm_g):
    rows, d = mem2.shape
    depth = mem_w_kv.shape[0]
    out = jax.ShapeDtypeStruct((depth, rows, MEM_W), BF16)
    return pl.pallas_call(
        _mem_kv_kernel,
        grid=(depth,),
        in_specs=[
            pl.BlockSpec((rows, d), lambda l: (0, 0)),
            pl.BlockSpec((None, 1, d), lambda l: (l, 0, 0)),
            pl.BlockSpec((None, d, 2 * MEM_W), lambda l: (l, 0, 0)),
            pl.BlockSpec((None, 1, HEAD_DIM), lambda l: (l, 0, 0)),
        ],
        out_specs=[pl.BlockSpec((None, rows, MEM_W), lambda l: (l, 0, 0))] * 2,
        out_shape=[out, out],
        compiler_params=_params("parallel"),
        name="mem_kv",
    )(mem2, mem_norm_g, mem_w_kv, mem_k_norm_g)


def _mixer_a_kernel(x_ref, g_ref, w_ref, gv_ref, ws_ref, bs_ref, gq_ref, mk_ref, mv_ref,
                    mix_ref, mo_ref, h_ref, u_ref, v_ref):
    tm = x_ref.shape[0]
    mix_w = u_ref.shape[1]
    tn = PROJ_TN
    gw = mix_w // GMLP_GROUPS
    causal = (lax.broadcasted_iota(jnp.int32, (CHUNK, CHUNK), 0)
              >= lax.broadcasted_iota(jnp.int32, (CHUNK, CHUNK), 1))
    w_s = [jnp.where(causal, ws_ref[grp], 0.0).astype(BF16) for grp in range(GMLP_GROUPS)]
    h_ref[...] = _rms(x_ref[...], g_ref[...]).astype(BF16)
    for c in range(mix_w // tn):
        cols = slice(c * tn, (c + 1) * tn)
        u_ref[:, cols] = _gelu(jnp.dot(h_ref[...], w_ref[:, cols], preferred_element_type=F32))
        v_ref[:, cols] = _gelu(jnp.dot(h_ref[...], w_ref[:, mix_w + c * tn:mix_w + (c + 1) * tn],
                                       preferred_element_type=F32))
    zq = jnp.dot(h_ref[...], w_ref[:, 2 * mix_w:], preferred_element_type=F32)
    _mem_attn(zq, gq_ref[...], mk_ref, mv_ref, mo_ref, slice(0, tm))
    for t in range(0, tm, CHUNK):
        chunk = slice(t, t + CHUNK)
        vn = _rms(v_ref[chunk, :], gv_ref[...]).astype(BF16)
        for grp in range(GMLP_GROUPS):
            cols = slice(grp * gw, (grp + 1) * gw)
            mixed = (jnp.dot(w_s[grp], vn[:, cols], preferred_element_type=F32)
                     + bs_ref[:, grp:grp + 1])
            mix_ref[chunk, cols] = (u_ref[chunk, cols] * mixed).astype(BF16)


def _mixer_a(x, norm_g, l, w_in, v_norm_g, w_s, b_s_t, mem_q_norm_g, mk, mv, seq):
    n, d = x.shape
    mix_w = v_norm_g.shape[-1]
    tm = PROJ_TM
    blocks_per_seq = seq // tm
    mem_len = mk.shape[1] // (n // seq)
    mem_spec = pl.BlockSpec((None, mem_len, MEM_W), lambda r: (l, r // blocks_per_seq, 0))
    return pl.pallas_call(
        _mixer_a_kernel,
        grid=(n // tm,),
        in_specs=[
            pl.BlockSpec((tm, d), lambda r: (r, 0)),
            pl.BlockSpec((None, None, 1, d), lambda r: (l, 1, 0, 0)),
            pl.BlockSpec((None, d, 2 * mix_w + MEM_W), lambda r: (l, 0, 0)),
            pl.BlockSpec((None, 1, mix_w), lambda r: (l, 0, 0)),
            pl.BlockSpec((None, GMLP_GROUPS, CHUNK, CHUNK), lambda r: (l, 0, 0, 0)),
            pl.BlockSpec((None, CHUNK, GMLP_GROUPS), lambda r: (l, 0, 0)),
            pl.BlockSpec((None, 1, HEAD_DIM), lambda r: (l, 0, 0)),
            mem_spec, mem_spec,
        ],
        out_specs=[pl.BlockSpec((tm, mix_w), lambda r: (r, 0)),
                   pl.BlockSpec((tm, MEM_W), lambda r: (r, 0))],
        out_shape=[jax.ShapeDtypeStruct((n, mix_w), BF16),
                   jax.ShapeDtypeStruct((n, MEM_W), BF16)],
        scratch_shapes=[pltpu.VMEM((tm, d), BF16),
                        pltpu.VMEM((tm, mix_w), F32),
                        pltpu.VMEM((tm, mix_w), F32)],
        compiler_params=_params("parallel"),
        name="mixer_a",
    )(x, norm_g, w_in, v_norm_g, w_s, b_s_t, mem_q_norm_g, mk, mv)


def _kv_proj_kernel(x_ref, g_ref, w_ref, gk_ref, cos_ref, sin_ref, k_ref, vt_ref, h_ref, v_ref):
    qk_w = k_ref.shape[1]
    heads, vd, tm = vt_ref.shape
    tn = PROJ_TN
    gk = gk_ref[...]
    for rows in _sub_blocks(tm):
        h_ref[rows, :] = _rms(x_ref[rows, :], g_ref[...]).astype(BF16)
    for rows in _sub_blocks(tm):
        cos, sin = cos_ref[rows, :], sin_ref[rows, :]
        for c in range(qk_w // tn):
            z = jnp.dot(h_ref[rows, :], w_ref[:, c * tn:(c + 1) * tn], preferred_element_type=F32)
            for k in range(tn // HEAD_DIM):
                r = _rms(z[:, k * HEAD_DIM:(k + 1) * HEAD_DIM], gk)
                k_ref[rows, c * tn + k * HEAD_DIM:c * tn + (k + 1) * HEAD_DIM] = (
                    _rope(r, cos, sin).astype(BF16))
        for hh in range(heads):
            v_ref[rows, :] = jnp.dot(h_ref[rows, :], w_ref[:, qk_w + hh * vd:qk_w + (hh + 1) * vd],
                                     preferred_element_type=F32)
            vt_ref[hh, :, rows] = v_ref[rows, :].T.astype(BF16)


def _kv_proj(x, kv_norm_g, w_kv, k_norm_g, cos, sin_signed, batch, seq, qk_w, vd):
    n, d = x.shape
    heads = (w_kv.shape[1] - qk_w) // vd
    tm = ATTN_T
    blocks_per_seq = seq // tm
    rope_spec = pl.BlockSpec((tm, HEAD_DIM), lambda r: (r % blocks_per_seq, 0))
    return pl.pallas_call(
        _kv_proj_kernel,
        grid=(n // tm,),
        in_specs=[
            pl.BlockSpec((tm, d), lambda r: (r, 0)),
            pl.BlockSpec((1, d), lambda r: (0, 0)),
            pl.BlockSpec((d, qk_w + heads * vd), lambda r: (0, 0)),
            pl.BlockSpec((1, HEAD_DIM), lambda r: (0, 0)),
            rope_spec, rope_spec,
        ],
        out_specs=[pl.BlockSpec((tm, qk_w), lambda r: (r, 0)),
                   pl.BlockSpec((None, None, heads, vd, tm),
                                lambda r: (r // blocks_per_seq, r % blocks_per_seq, 0, 0, 0))],
        out_shape=[jax.ShapeDtypeStruct((n, qk_w), BF16),
                   jax.ShapeDtypeStruct((batch, blocks_per_seq, heads, vd, tm), BF16)],
        scratch_shapes=[pltpu.VMEM((tm, d), BF16), pltpu.VMEM((tm, vd), F32)],
        compiler_params=_params("parallel"),
        name="kv_proj",
    )(x, kv_norm_g, w_kv, k_norm_g, cos, sin_signed)


def _proj_b_kernel(x_ref, g_ref, w_ref, gqn_ref, cos_ref, sin_ref, gq_ref, mk_ref, mv_ref,
                   q_ref, mo_ref, h_ref):
    tm, qk_w = q_ref.shape
    tn = PROJ_TN
    gqn = gqn_ref[...]
    for rows in _sub_blocks(tm):
        h_ref[rows, :] = _rms(x_ref[rows, :], g_ref[...]).astype(BF16)
    for rows in _sub_blocks(tm):
        cos, sin = cos_ref[rows, :], sin_ref[rows, :]
        zq = jnp.dot(h_ref[rows, :], w_ref[:, qk_w:], preferred_element_type=F32)
        _mem_attn(zq, gq_ref[...], mk_ref, mv_ref, mo_ref, rows)
        for c in range(qk_w // tn):
            z = jnp.dot(h_ref[rows, :], w_ref[:, c * tn:(c + 1) * tn], preferred_element_type=F32)
            for k in range(tn // HEAD_DIM):
                r = _rms(z[:, k * HEAD_DIM:(k + 1) * HEAD_DIM], gqn)
                q_ref[rows, c * tn + k * HEAD_DIM:c * tn + (k + 1) * HEAD_DIM] = (
                    _rope(r, cos, sin) * Q_PRESCALE).astype(BF16)


def _proj_b(x, norm_g, l, w_in, j, q_norm_g, cos, sin_signed, mem_q_norm_g, mk, mv, seq):
    n, d = x.shape
    qk_w = w_in.shape[2] - MEM_W
    tm = PROJ_TM
    blocks_per_seq = seq // tm
    mem_len = mk.shape[1] // (n // seq)
    rope_spec = pl.BlockSpec((tm, HEAD_DIM), lambda r: (r % blocks_per_seq, 0))
    mem_spec = pl.BlockSpec((None, mem_len, MEM_W), lambda r: (l, r // blocks_per_seq, 0))
    return pl.pallas_call(
        _proj_b_kernel,
        grid=(n // tm,),
        in_specs=[
            pl.BlockSpec((tm, d), lambda r: (r, 0)),
            pl.BlockSpec((None, None, 1, d), lambda r: (l, 1, 0, 0)),
            pl.BlockSpec((None, d, qk_w + MEM_W), lambda r: (j, 0, 0)),
            pl.BlockSpec((None, 1, HEAD_DIM), lambda r: (j, 0, 0)),
            rope_spec, rope_spec,
            pl.BlockSpec((None, 1, HEAD_DIM), lambda r: (l, 0, 0)),
            mem_spec, mem_spec,
        ],
        out_specs=[pl.BlockSpec((tm, qk_w), lambda r: (r, 0)),
                   pl.BlockSpec((tm, MEM_W), lambda r: (r, 0))],
        out_shape=[jax.ShapeDtypeStruct((n, qk_w), BF16),
                   jax.ShapeDtypeStruct((n, MEM_W), BF16)],
        scratch_shapes=[pltpu.VMEM((tm, d), BF16)],
        compiler_params=_params("parallel"),
        name="proj_b",
    )(x, norm_g, w_in, q_norm_g, cos, sin_signed, mem_q_norm_g, mk, mv)


def _diff_attn_kernel(lam_ref, gs_ref, q0_ref, q1_ref, k0_ref, k1_ref, vt_ref, o_ref, acc_ref,
                      *, lam_init):
    t = ATTN_T
    qi = pl.program_id(2)
    q_refs, k_refs = (q0_ref, q1_ref), (k0_ref, k1_ref)
    acc_ref[...] = jnp.zeros(acc_ref.shape, F32)

    def scores(ki, c):
        rows = pl.ds(pl.multiple_of(ki * t, t), t)
        return lax.dot_general(k_refs[c][rows, :], q_refs[c][...], NT_DIMS,
                               preferred_element_type=F32)

    def update(ki, c, st, stat, masked):
        m_old, l_old = stat
        if masked:
            key_le_query = (lax.broadcasted_iota(jnp.int32, (t, t), 0)
                            <= lax.broadcasted_iota(jnp.int32, (t, t), 1))
            st = jnp.where(key_le_query, st, -jnp.inf)
        m_new = jnp.maximum(m_old, jnp.max(st, axis=0, keepdims=True))
        alpha = jnp.exp2(m_old - m_new)
        pt = jnp.exp2(st - m_new)
        acc_ref[c] = alpha * acc_ref[c] + jnp.dot(vt_ref[ki], pt.astype(BF16),
                                                  preferred_element_type=F32)
        return m_new, alpha * l_old + jnp.sum(pt, axis=0, keepdims=True)

    def blocks(kis, stats, masked_last):
        stats = list(stats)
        work = [(ki, c, masked_last and n == len(kis) - 1)
                for n, ki in enumerate(kis) for c in range(2)]
        pending = [scores(ki, c) for ki, c, _ in work[:2]]
        for n, (ki, c, masked) in enumerate(work):
            if n + 2 < len(work):
                pending.append(scores(work[n + 2][0], work[n + 2][1]))
            stats[c] = update(ki, c, pending[n], stats[c], masked)
        return tuple(stats)

    init = ((jnp.full((1, t), -jnp.inf, F32), jnp.zeros((1, t), F32)),) * 2
    stats = lax.fori_loop(0, qi // 2, lambda j, s: blocks((2 * j, 2 * j + 1), s, False), init)
    stats = lax.cond(qi % 2 == 1,
                     lambda s: blocks((qi - 1, qi), s, True),
                     lambda s: blocks((qi,), s, True), stats)
    l0, l1 = stats[0][1], stats[1][1]

    lp = lam_ref[...]
    lam = (jnp.exp(jnp.sum(lp[0:1] * lp[1:2], axis=-1, keepdims=True))
           - jnp.exp(jnp.sum(lp[2:3] * lp[3:4], axis=-1, keepdims=True)) + lam_init)
    ot = acc_ref[0] / l0 - lam * (acc_ref[1] / l1)
    ot = ot * lax.rsqrt(jnp.mean(ot * ot, axis=0, keepdims=True) + EPS)
    o_ref[...] = (ot.T * gs_ref[...] * (1.0 - lam_init)).astype(BF16)


def _diff_attn(q, k, vt, b_lambda, subln_g, j, lam_init, batch, seq):
    n, qk_w = q.shape
    heads = qk_w // (2 * HEAD_DIM)
    _, nk, _, vd, t = vt.shape
    q3 = q.reshape(batch, seq, qk_w)
    k3 = k.reshape(batch, seq, qk_w)
    o = pl.pallas_call(
        functools.partial(_diff_attn_kernel, lam_init=lam_init),
        grid=(batch, heads, seq // t),
        in_specs=[
            pl.BlockSpec((None, 4, HEAD_DIM), lambda b, h, i: (j, 0, 0)),
            pl.BlockSpec((None, 1, vd), lambda b, h, i: (j, 0, 0)),
            pl.BlockSpec((None, t, HEAD_DIM), lambda b, h, i: (b, i, h)),
            pl.BlockSpec((None, t, HEAD_DIM), lambda b, h, i: (b, i, heads + h)),
            pl.BlockSpec((None, seq, HEAD_DIM), lambda b, h, i: (b, 0, h)),
            pl.BlockSpec((None, seq, HEAD_DIM), lambda b, h, i: (b, 0, heads + h)),
            pl.BlockSpec((None, nk, None, vd, t), lambda b, h, i: (b, 0, h, 0, 0)),
        ],
        out_specs=pl.BlockSpec((None, t, vd), lambda b, h, i: (b, i, h)),
        out_shape=jax.ShapeDtypeStruct((batch, seq, heads * vd), BF16),
        scratch_shapes=[pltpu.VMEM((2, vd, t), F32)],
        compiler_params=_params("parallel", "parallel", "parallel"),
        name="diff_attn",
    )(b_lambda, subln_g, q3, q3, k3, k3, vt)
    return o.reshape(n, heads * vd)


def _out_proj_kernel(x_ref, mix_ref, mo_ref, w_ref, o_ref):
    mix_w = mix_ref.shape[1]
    tn = PROJ_TN
    for c in range(o_ref.shape[1] // tn):
        cols = slice(c * tn, (c + 1) * tn)
        y = (jnp.dot(mix_ref[...], w_ref[:mix_w, cols], preferred_element_type=F32)
             + jnp.dot(mo_ref[...], w_ref[mix_w:, cols], preferred_element_type=F32))
        o_ref[:, cols] = x_ref[:, cols] + y


def _out_proj(x, mix, mo, w_out, l):
    n, d = x.shape
    mix_w = mix.shape[1]
    tm = OUT_TM
    return pl.pallas_call(
        _out_proj_kernel,
        grid=(n // tm,),
        in_specs=[
            pl.BlockSpec((tm, d), lambda r: (r, 0)),
            pl.BlockSpec((tm, mix_w), lambda r: (r, 0)),
            pl.BlockSpec((tm, MEM_W), lambda r: (r, 0)),
            pl.BlockSpec((None, d, d), lambda r: (l, 0, 0)),
        ],
        out_specs=pl.BlockSpec((tm, d), lambda r: (r, 0)),
        out_shape=jax.ShapeDtypeStruct((n, d), F32),
        compiler_params=_params("parallel"),
        name="out_proj",
    )(x, mix, mo, w_out)


def kernel(x, mem, norm_g, ffn_w_gu, ffn_w_down, w_out, mem_norm_g, mem_w_kv, mem_q_norm_g,
           mem_k_norm_g, a_w_in, a_v_norm_g, a_w_s, a_b_s, kv_norm_g, w_kv, k_norm_g, b_w_in,
           b_q_norm_g, b_lambda, b_subln_g):
    batch, seq, d = x.shape
    depth = norm_g.shape[0]
    n_a = a_w_in.shape[0]
    mix_w = d - MEM_W
    n = batch * seq

    w_out, mem_w_kv, a_w_in, w_kv, b_w_in = (
        w.astype(BF16) for w in (w_out, mem_w_kv, a_w_in, w_kv, b_w_in))
    ffn_w = (ffn_w_gu[0, 0].astype(BF16), ffn_w_down[0, 0].astype(BF16))

    def ffn(x, l, i, ffn_w):
        last = (l == depth - 1 and i == 1)
        nxt = None if last else (ffn_w_gu, ffn_w_down) + ((l, 1) if i == 0 else (l + 1, 0))
        res = _ffn(x, norm_g, l, 2 * i, ffn_w[0], ffn_w[1], nxt)
        return (res[0], None) if last else (res[0], (res[1], res[2]))

    norm_g = norm_g[:, :, None, :]
    mem_norm_g = mem_norm_g[:, None, :]
    mem_q_norm_g = mem_q_norm_g[:, None, :]
    mem_k_norm_g = mem_k_norm_g[:, None, :]
    a_v_norm_g = a_v_norm_g[:, None, :]
    b_q_norm_g = b_q_norm_g[:, None, :]
    b_subln_g = b_subln_g[:, None, :]
    a_b_s_t = jnp.swapaxes(a_b_s, 1, 2)

    pos = jnp.arange(seq, dtype=F32)
    inv = ROPE_THETA ** (-jnp.arange(0, HEAD_DIM, 2, dtype=F32) / HEAD_DIM)
    ang = pos[:, None] * inv[None, :]
    ang = jnp.concatenate([ang, ang], axis=-1)
    cos = jnp.cos(ang)
    sign = jnp.where(jnp.arange(HEAD_DIM) < HEAD_DIM // 2, -1.0, 1.0).astype(F32)
    sin_signed = jnp.sin(ang) * sign[None, :]

    x = x.reshape(n, d)
    mk, mv = _mem_kv(mem.reshape(batch * mem.shape[1], d), mem_norm_g, mem_w_kv, mem_k_norm_g)

    k_sh = v_sh = None
    for l in range(depth):
        if l == n_a:
            k_sh, v_sh = _kv_proj(x, kv_norm_g[None, :], w_kv, k_norm_g[None, :], cos, sin_signed,
                                  batch, seq, mix_w, b_subln_g.shape[-1])
        x, ffn_w = ffn(x, l, 0, ffn_w)
        if l < n_a:
            mix, mo = _mixer_a(x, norm_g, l, a_w_in, a_v_norm_g, a_w_s, a_b_s_t, mem_q_norm_g,
                               mk, mv, seq)
        else:
            j = l - n_a
            q, mo = _proj_b(x, norm_g, l, b_w_in, j, b_q_norm_g, cos, sin_signed, mem_q_norm_g,
                            mk, mv, seq)
            lam_init = 0.8 - 0.6 * math.exp(-0.3 * l)
            mix = _diff_attn(q, k_sh, v_sh, b_lambda, b_subln_g, j, lam_init, batch, seq)
        x = _out_proj(x, mix, mo, w_out, l)
        x, ffn_w = ffn(x, l, 1, ffn_w)
    return x.reshape(batch, seq, d)
```

```python
import functools
import math

import jax
import jax.numpy as jnp
from jax import lax
from jax.experimental import pallas as pl
from jax.experimental.pallas import tpu as pltpu

EPS = 1e-6
HEAD_DIM = 128
MEM_HEADS = 4
MEM_W = MEM_HEADS * HEAD_DIM
CHUNK = 128
GMLP_GROUPS = 6
ROPE_THETA = 10000.0
ATTN_SCALE = HEAD_DIM ** -0.5
SQRT_HALF = 0.7071067811865476

V7X_VMEM_BYTES = 64 * 1024 * 1024
VMEM_LIMIT_BYTES = V7X_VMEM_BYTES - 8 * 1024 * 1024
FFN_VMEM_LIMIT_BYTES = V7X_VMEM_BYTES - 5 * 1024 * 1024
FFN_TM = 1024
FFN_TF = 512
PROJ_TM = 512
PROJ_B_TM = 256
PROJ_SUB = 256
PROJ_TN = 512
OUT_TM = 512
ATTN_T = 512
LOG2E = 1.4426950408889634
Q_PRESCALE = ATTN_SCALE * LOG2E

F32 = jnp.float32
BF16 = jnp.bfloat16
NT_DIMS = (((1,), (1,)), ((), ()))


def _params(*semantics):
    return pltpu.CompilerParams(dimension_semantics=semantics,
                                vmem_limit_bytes=VMEM_LIMIT_BYTES)


def _rms(x, g):
    return x * lax.rsqrt(jnp.mean(x * x, axis=-1, keepdims=True) + EPS) * g


def _gelu(z):
    return 0.5 * z * (1.0 + lax.erf(z * SQRT_HALF))


def _rope(r, cos, sin_signed):
    return r * cos + pltpu.roll(r, HEAD_DIM // 2, 1) * sin_signed


def _mem_attn(zq, gq, mk_ref, mv_ref, mo_ref, rows):
    for hh in range(MEM_HEADS):
        cols = slice(hh * HEAD_DIM, (hh + 1) * HEAD_DIM)
        qh = _rms(zq[:, cols], gq).astype(BF16)
        s = lax.dot_general(qh, mk_ref[:, cols], NT_DIMS, preferred_element_type=F32) * ATTN_SCALE
        e = jnp.exp(s - jnp.max(s, axis=-1, keepdims=True))
        p = e / jnp.sum(e, axis=-1, keepdims=True)
        mo_ref[rows, cols] = jnp.dot(p.astype(BF16), mv_ref[:, cols],
                                     preferred_element_type=F32).astype(BF16)


def _sub_blocks(tm):
    return [slice(s, s + PROJ_SUB) for s in range(0, tm, PROJ_SUB)]


def _ffn_kernel(x_ref, g_ref, wgu_ref, wd_ref, *refs):
    tf = wd_ref.shape[0]
    if len(refs) == 2:
        cast = None
        o_ref, h_ref = refs
    else:
        cast = refs[:3] + refs[4:6]
        o_ref, h_ref = refs[3], refs[6]
    j = pl.program_id(1)

    def step(first):
        if first:
            h_ref[...] = _rms(x_ref[...], g_ref[...]).astype(BF16)
        h = h_ref[...]
        g = jnp.dot(h, wgu_ref[:, :tf], preferred_element_type=F32)
        u = jnp.dot(h, wgu_ref[:, tf:], preferred_element_type=F32)
        if cast is not None:
            g32_ref, u32_ref, d32_ref, gu16_ref, d16_ref = cast
            gu16_ref[:, :tf] = g32_ref[...].astype(BF16)
            gu16_ref[:, tf:] = u32_ref[...].astype(BF16)
            d16_ref[...] = d32_ref[...].astype(BF16)
        act = (0.5 * g * jax.nn.sigmoid(g)) * u
        down = jnp.dot(act.astype(BF16), wd_ref[...], preferred_element_type=F32)
        if first:
            o_ref[...] = x_ref[...] + down
        else:
            o_ref[...] += down

    pl.when(j == 0)(functools.partial(step, True))
    pl.when(j > 0)(functools.partial(step, False))


def _chunk_major(w_gu, tf):
    d, f2 = w_gu.shape
    nf = f2 // (2 * tf)
    return w_gu.reshape(d, 2, nf, tf).transpose(2, 0, 1, 3).reshape(nf, d, 2 * tf)


def _ffn(x, norm_g, l, slot, w_gu, w_down, next_w=None):
    n, d = x.shape
    f = w_down.shape[0]
    tm, tf = FFN_TM, FFN_TF
    nr, nf = n // tm, f // tf
    in_specs = [
        pl.BlockSpec((tm, d), lambda r, j: (r, 0)),
        pl.BlockSpec((None, None, 1, d), lambda r, j: (l, slot, 0, 0)),
        pl.BlockSpec((None, d, 2 * tf), lambda r, j: (j, 0, 0)),
        pl.BlockSpec((tf, d), lambda r, j: (j, 0)),
    ]
    out_specs = [pl.BlockSpec((tm, d), lambda r, j: (r, 0))]
    out_shape = [jax.ShapeDtypeStruct((n, d), F32)]
    args = [x, norm_g, w_gu, w_down]
    if next_w is not None:
        gu32, down32, l2, i2 = next_w
        steps = nr * nf
        assert d % nr == 0 and f % steps == 0
        in_specs += [
            pl.BlockSpec((None, None, d // nr, tf), lambda r, j: (l2, i2, r, j)),
            pl.BlockSpec((None, None, d // nr, tf), lambda r, j: (l2, i2, r, j + nf)),
            pl.BlockSpec((None, None, f // steps, d), lambda r, j: (l2, i2, r * nf + j, 0)),
        ]
        out_specs += [pl.BlockSpec((None, d // nr, 2 * tf), lambda r, j: (j, r, 0)),
                      pl.BlockSpec((f // steps, d), lambda r, j: (r * nf + j, 0))]
        out_shape += [jax.ShapeDtypeStruct((nf, d, 2 * tf), BF16),
                      jax.ShapeDtypeStruct((f, d), BF16)]
        args += [gu32, gu32, down32]
    return pl.pallas_call(
        _ffn_kernel,
        grid=(nr, nf),
        in_specs=in_specs,
        out_specs=out_specs,
        out_shape=out_shape,
        scratch_shapes=[pltpu.VMEM((tm, d), BF16)],
        compiler_params=pltpu.CompilerParams(dimension_semantics=("parallel", "arbitrary"),
                                             vmem_limit_bytes=FFN_VMEM_LIMIT_BYTES),
        name="ffn",
    )(*args)


def _mem_kv_kernel(mem_ref, g_ref, w_ref, gk_ref, mk_ref, mv_ref):
    h = _rms(mem_ref[...], g_ref[...]).astype(BF16)
    kv = jnp.dot(h, w_ref[...], preferred_element_type=F32)
    gk = gk_ref[...]
    for hh in range(MEM_HEADS):
        cols = slice(hh * HEAD_DIM, (hh + 1) * HEAD_DIM)
        mk_ref[:, cols] = _rms(kv[:, cols], gk).astype(BF16)
    mv_ref[...] = kv[:, MEM_W:].astype(BF16)


def _mem_kv(mem2, mem_norm_g, mem_w_kv, mem_k_norm_g):
    rows, d = mem2.shape
    depth = mem_w_kv.shape[0]
    out = jax.ShapeDtypeStruct((depth, rows, MEM_W), BF16)
    return pl.pallas_call(
        _mem_kv_kernel,
        grid=(depth,),
        in_specs=[
            pl.BlockSpec((rows, d), lambda l: (0, 0)),
            pl.BlockSpec((None, 1, d), lambda l: (l, 0, 0)),
            pl.BlockSpec((None, d, 2 * MEM_W), lambda l: (l, 0, 0)),
            pl.BlockSpec((None, 1, HEAD_DIM), lambda l: (l, 0, 0)),
        ],
        out_specs=[pl.BlockSpec((None, rows, MEM_W), lambda l: (l, 0, 0))] * 2,
        out_shape=[out, out],
        compiler_params=_params("parallel"),
        name="mem_kv",
    )(mem2, mem_norm_g, mem_w_kv, mem_k_norm_g)


def _mixer_a_kernel(x_ref, g_ref, w_ref, gv_ref, ws_ref, bs_ref, gq_ref, mk_ref, mv_ref,
                    mix_ref, mo_ref, h_ref, u_ref, v_ref):
    tm = x_ref.shape[0]
    mix_w = u_ref.shape[1]
    tn = PROJ_TN
    gw = mix_w // GMLP_GROUPS
    causal = (lax.broadcasted_iota(jnp.int32, (CHUNK, CHUNK), 0)
              >= lax.broadcasted_iota(jnp.int32, (CHUNK, CHUNK), 1))
    w_s = [jnp.where(causal, ws_ref[grp], 0.0).astype(BF16) for grp in range(GMLP_GROUPS)]
    h_ref[...] = _rms(x_ref[...], g_ref[...]).astype(BF16)
    for c in range(mix_w // tn):
        cols = slice(c * tn, (c + 1) * tn)
        u_ref[:, cols] = _gelu(jnp.dot(h_ref[...], w_ref[:, cols], preferred_element_type=F32))
        v_ref[:, cols] = _gelu(jnp.dot(h_ref[...], w_ref[:, mix_w + c * tn:mix_w + (c + 1) * tn],
                                       preferred_element_type=F32))
    zq = jnp.dot(h_ref[...], w_ref[:, 2 * mix_w:], preferred_element_type=F32)
    _mem_attn(zq, gq_ref[...], mk_ref, mv_ref, mo_ref, slice(0, tm))
    for t in range(0, tm, CHUNK):
        chunk = slice(t, t + CHUNK)
        vn = _rms(v_ref[chunk, :], gv_ref[...]).astype(BF16)
        for grp in range(GMLP_GROUPS):
            cols = slice(grp * gw, (grp + 1) * gw)
            mixed = (jnp.dot(w_s[grp], vn[:, cols], preferred_element_type=F32)
                     + bs_ref[:, grp:grp + 1])
            mix_ref[chunk, cols] = (u_ref[chunk, cols] * mixed).astype(BF16)


def _mixer_a(x, norm_g, l, w_in, v_norm_g, w_s, b_s_t, mem_q_norm_g, mk, mv, seq):
    n, d = x.shape
    mix_w = v_norm_g.shape[-1]
    tm = PROJ_TM
    blocks_per_seq = seq // tm
    mem_len = mk.shape[1] // (n // seq)
    mem_spec = pl.BlockSpec((None, mem_len, MEM_W), lambda r: (l, r // blocks_per_seq, 0))
    return pl.pallas_call(
        _mixer_a_kernel,
        grid=(n // tm,),
        in_specs=[
            pl.BlockSpec((tm, d), lambda r: (r, 0)),
            pl.BlockSpec((None, None, 1, d), lambda r: (l, 1, 0, 0)),
            pl.BlockSpec((None, d, 2 * mix_w + MEM_W), lambda r: (l, 0, 0)),
            pl.BlockSpec((None, 1, mix_w), lambda r: (l, 0, 0)),
            pl.BlockSpec((None, GMLP_GROUPS, CHUNK, CHUNK), lambda r: (l, 0, 0, 0)),
            pl.BlockSpec((None, CHUNK, GMLP_GROUPS), lambda r: (l, 0, 0)),
            pl.BlockSpec((None, 1, HEAD_DIM), lambda r: (l, 0, 0)),
            mem_spec, mem_spec,
        ],
        out_specs=[pl.BlockSpec((tm, mix_w), lambda r: (r, 0)),
                   pl.BlockSpec((tm, MEM_W), lambda r: (r, 0))],
        out_shape=[jax.ShapeDtypeStruct((n, mix_w), BF16),
                   jax.ShapeDtypeStruct((n, MEM_W), BF16)],
        scratch_shapes=[pltpu.VMEM((tm, d), BF16),
                        pltpu.VMEM((tm, mix_w), F32),
                        pltpu.VMEM((tm, mix_w), F32)],
        compiler_params=_params("parallel"),
        name="mixer_a",
    )(x, norm_g, w_in, v_norm_g, w_s, b_s_t, mem_q_norm_g, mk, mv)


def _kv_proj_kernel(x_ref, g_ref, w_ref, gk_ref, cos_ref, sin_ref, k_ref, vt_ref, h_ref, v_ref):
    qk_w = k_ref.shape[1]
    heads, vd, tm = vt_ref.shape
    tn = PROJ_TN
    gk = gk_ref[...]
    for rows in _sub_blocks(tm):
        h_ref[rows, :] = _rms(x_ref[rows, :], g_ref[...]).astype(BF16)
    for rows in _sub_blocks(tm):
        cos, sin = cos_ref[rows, :], sin_ref[rows, :]
        for c in range(qk_w // tn):
            z = jnp.dot(h_ref[rows, :], w_ref[:, c * tn:(c + 1) * tn], preferred_element_type=F32)
            for k in range(tn // HEAD_DIM):
                r = _rms(z[:, k * HEAD_DIM:(k + 1) * HEAD_DIM], gk)
                k_ref[rows, c * tn + k * HEAD_DIM:c * tn + (k + 1) * HEAD_DIM] = (
                    _rope(r, cos, sin).astype(BF16))
        for hh in range(heads):
            v_ref[rows, :] = jnp.dot(h_ref[rows, :], w_ref[:, qk_w + hh * vd:qk_w + (hh + 1) * vd],
                                     preferred_element_type=F32)
            vt_ref[hh, :, rows] = v_ref[rows, :].T.astype(BF16)


def _kv_proj(x, kv_norm_g, w_kv, k_norm_g, cos, sin_signed, batch, seq, qk_w, vd):
    n, d = x.shape
    heads = (w_kv.shape[1] - qk_w) // vd
    tm = ATTN_T
    blocks_per_seq = seq // tm
    rope_spec = pl.BlockSpec((tm, HEAD_DIM), lambda r: (r % blocks_per_seq, 0))
    return pl.pallas_call(
        _kv_proj_kernel,
        grid=(n // tm,),
        in_specs=[
            pl.BlockSpec((tm, d), lambda r: (r, 0)),
            pl.BlockSpec((1, d), lambda r: (0, 0)),
            pl.BlockSpec((d, qk_w + heads * vd), lambda r: (0, 0)),
            pl.BlockSpec((1, HEAD_DIM), lambda r: (0, 0)),
            rope_spec, rope_spec,
        ],
        out_specs=[pl.BlockSpec((tm, qk_w), lambda r: (r, 0)),
                   pl.BlockSpec((None, None, heads, vd, tm),
                                lambda r: (r // blocks_per_seq, r % blocks_per_seq, 0, 0, 0))],
        out_shape=[jax.ShapeDtypeStruct((n, qk_w), BF16),
                   jax.ShapeDtypeStruct((batch, blocks_per_seq, heads, vd, tm), BF16)],
        scratch_shapes=[pltpu.VMEM((tm, d), BF16), pltpu.VMEM((tm, vd), F32)],
        compiler_params=_params("parallel"),
        name="kv_proj",
    )(x, kv_norm_g, w_kv, k_norm_g, cos, sin_signed)


def _proj_b_kernel(x_ref, g_ref, w_ref, gqn_ref, cos_ref, sin_ref, gq_ref, mk_ref, mv_ref,
                   q_ref, mo_ref, h_ref):
    tm, qk_w = q_ref.shape
    tn = PROJ_TN
    gqn = gqn_ref[...]
    for rows in _sub_blocks(tm):
        h_ref[rows, :] = _rms(x_ref[rows, :], g_ref[...]).astype(BF16)
    for rows in _sub_blocks(tm):
        cos, sin = cos_ref[rows, :], sin_ref[rows, :]
        zq = jnp.dot(h_ref[rows, :], w_ref[:, qk_w:], preferred_element_type=F32)
        _mem_attn(zq, gq_ref[...], mk_ref, mv_ref, mo_ref, rows)
        for c in range(qk_w // tn):
            z = jnp.dot(h_ref[rows, :], w_ref[:, c * tn:(c + 1) * tn], preferred_element_type=F32)
            for k in range(tn // HEAD_DIM):
                r = _rms(z[:, k * HEAD_DIM:(k + 1) * HEAD_DIM], gqn)
                q_ref[rows, c * tn + k * HEAD_DIM:c * tn + (k + 1) * HEAD_DIM] = (
                    _rope(r, cos, sin) * Q_PRESCALE).astype(BF16)


def _proj_b(x, norm_g, l, w_in, j, q_norm_g, cos, sin_signed, mem_q_norm_g, mk, mv, seq):
    n, d = x.shape
    qk_w = w_in.shape[2] - MEM_W
    tm = PROJ_B_TM
    blocks_per_seq = seq // tm
    mem_len = mk.shape[1] // (n // seq)
    rope_spec = pl.BlockSpec((tm, HEAD_DIM), lambda r: (r % blocks_per_seq, 0))
    mem_spec = pl.BlockSpec((None, mem_len, MEM_W), lambda r: (l, r // blocks_per_seq, 0))
    return pl.pallas_call(
        _proj_b_kernel,
        grid=(n // tm,),
        in_specs=[
            pl.BlockSpec((tm, d), lambda r: (r, 0)),
            pl.BlockSpec((None, None, 1, d), lambda r: (l, 1, 0, 0)),
            pl.BlockSpec((None, d, qk_w + MEM_W), lambda r: (j, 0, 0)),
            pl.BlockSpec((None, 1, HEAD_DIM), lambda r: (j, 0, 0)),
            rope_spec, rope_spec,
            pl.BlockSpec((None, 1, HEAD_DIM), lambda r: (l, 0, 0)),
            mem_spec, mem_spec,
        ],
        out_specs=[pl.BlockSpec((tm, qk_w), lambda r: (r, 0)),
                   pl.BlockSpec((tm, MEM_W), lambda r: (r, 0))],
        out_shape=[jax.ShapeDtypeStruct((n, qk_w), BF16),
                   jax.ShapeDtypeStruct((n, MEM_W), BF16)],
        scratch_shapes=[pltpu.VMEM((tm, d), BF16)],
        compiler_params=_params("parallel"),
        name="proj_b",
    )(x, norm_g, w_in, q_norm_g, cos, sin_signed, mem_q_norm_g, mk, mv)


def _diff_attn_kernel(lam_ref, gs_ref, q0_ref, q1_ref, k0_ref, k1_ref, vt_ref, o_ref, acc_ref,
                      *, lam_init):
    t = ATTN_T
    qi = pl.program_id(2)
    q_refs, k_refs = (q0_ref, q1_ref), (k0_ref, k1_ref)
    acc_ref[...] = jnp.zeros(acc_ref.shape, F32)

    def scores(ki, c):
        rows = pl.ds(pl.multiple_of(ki * t, t), t)
        return lax.dot_general(k_refs[c][rows, :], q_refs[c][...], NT_DIMS,
                               preferred_element_type=F32)

    def update(ki, c, st, stat, masked):
        m_old, l_old = stat
        if masked:
            key_le_query = (lax.broadcasted_iota(jnp.int32, (t, t), 0)
                            <= lax.broadcasted_iota(jnp.int32, (t, t), 1))
            st = jnp.where(key_le_query, st, -jnp.inf)
        m_new = jnp.maximum(m_old, jnp.max(st, axis=0, keepdims=True))
        alpha = jnp.exp2(m_old - m_new)
        pt = jnp.exp2(st - m_new)
        acc_ref[c] = alpha * acc_ref[c] + jnp.dot(vt_ref[ki], pt.astype(BF16),
                                                  preferred_element_type=F32)
        return m_new, alpha * l_old + jnp.sum(pt, axis=0, keepdims=True)

    def blocks(kis, stats, masked_last):
        stats = list(stats)
        work = [(ki, c, masked_last and n == len(kis) - 1)
                for n, ki in enumerate(kis) for c in range(2)]
        pending = [scores(ki, c) for ki, c, _ in work[:2]]
        for n, (ki, c, masked) in enumerate(work):
            if n + 2 < len(work):
                pending.append(scores(work[n + 2][0], work[n + 2][1]))
            stats[c] = update(ki, c, pending[n], stats[c], masked)
        return tuple(stats)

    init = ((jnp.full((1, t), -jnp.inf, F32), jnp.zeros((1, t), F32)),) * 2
    stats = lax.fori_loop(0, qi // 2, lambda j, s: blocks((2 * j, 2 * j + 1), s, False), init)
    stats = lax.cond(qi % 2 == 1,
                     lambda s: blocks((qi - 1, qi), s, True),
                     lambda s: blocks((qi,), s, True), stats)
    l0, l1 = stats[0][1], stats[1][1]

    lp = lam_ref[...]
    lam = (jnp.exp(jnp.sum(lp[0:1] * lp[1:2], axis=-1, keepdims=True))
           - jnp.exp(jnp.sum(lp[2:3] * lp[3:4], axis=-1, keepdims=True)) + lam_init)
    ot = acc_ref[0] / l0 - lam * (acc_ref[1] / l1)
    ot = ot * lax.rsqrt(jnp.mean(ot * ot, axis=0, keepdims=True) + EPS)
    o_ref[...] = (ot.T * gs_ref[...] * (1.0 - lam_init)).astype(BF16)


def _diff_attn(q, k, vt, b_lambda, subln_g, j, lam_init, batch, seq):
    n, qk_w = q.shape
    heads = qk_w // (2 * HEAD_DIM)
    _, nk, _, vd, t = vt.shape
    q3 = q.reshape(batch, seq, qk_w)
    k3 = k.reshape(batch, seq, qk_w)
    o = pl.pallas_call(
        functools.partial(_diff_attn_kernel, lam_init=lam_init),
        grid=(batch, heads, seq // t),
        in_specs=[
            pl.BlockSpec((None, 4, HEAD_DIM), lambda b, h, i: (j, 0, 0)),
            pl.BlockSpec((None, 1, vd), lambda b, h, i: (j, 0, 0)),
            pl.BlockSpec((None, t, HEAD_DIM), lambda b, h, i: (b, i, h)),
            pl.BlockSpec((None, t, HEAD_DIM), lambda b, h, i: (b, i, heads + h)),
            pl.BlockSpec((None, seq, HEAD_DIM), lambda b, h, i: (b, 0, h)),
            pl.BlockSpec((None, seq, HEAD_DIM), lambda b, h, i: (b, 0, heads + h)),
            pl.BlockSpec((None, nk, None, vd, t), lambda b, h, i: (b, 0, h, 0, 0)),
        ],
        out_specs=pl.BlockSpec((None, t, vd), lambda b, h, i: (b, i, h)),
        out_shape=jax.ShapeDtypeStruct((batch, seq, heads * vd), BF16),
        scratch_shapes=[pltpu.VMEM((2, vd, t), F32)],
        compiler_params=_params("parallel", "parallel", "parallel"),
        name="diff_attn",
    )(b_lambda, subln_g, q3, q3, k3, k3, vt)
    return o.reshape(n, heads * vd)


def _out_proj_kernel(x_ref, mix_ref, mo_ref, w_ref, o_ref):
    mix_w = mix_ref.shape[1]
    tn = PROJ_TN
    for c in range(o_ref.shape[1] // tn):
        cols = slice(c * tn, (c + 1) * tn)
        y = (jnp.dot(mix_ref[...], w_ref[:mix_w, cols], preferred_element_type=F32)
             + jnp.dot(mo_ref[...], w_ref[mix_w:, cols], preferred_element_type=F32))
        o_ref[:, cols] = x_ref[:, cols] + y


def _out_proj(x, mix, mo, w_out, l):
    n, d = x.shape
    mix_w = mix.shape[1]
    tm = OUT_TM
    return pl.pallas_call(
        _out_proj_kernel,
        grid=(n // tm,),
        in_specs=[
            pl.BlockSpec((tm, d), lambda r: (r, 0)),
            pl.BlockSpec((tm, mix_w), lambda r: (r, 0)),
            pl.BlockSpec((tm, MEM_W), lambda r: (r, 0)),
            pl.BlockSpec((None, d, d), lambda r: (l, 0, 0)),
        ],
        out_specs=pl.BlockSpec((tm, d), lambda r: (r, 0)),
        out_shape=jax.ShapeDtypeStruct((n, d), F32),
        compiler_params=_params("parallel"),
        name="out_proj",
    )(x, mix, mo, w_out)


def kernel(x, mem, norm_g, ffn_w_gu, ffn_w_down, w_out, mem_norm_g, mem_w_kv, mem_q_norm_g,
           mem_k_norm_g, a_w_in, a_v_norm_g, a_w_s, a_b_s, kv_norm_g, w_kv, k_norm_g, b_w_in,
           b_q_norm_g, b_lambda, b_subln_g):
    batch, seq, d = x.shape
    depth = norm_g.shape[0]
    n_a = a_w_in.shape[0]
    mix_w = d - MEM_W
    n = batch * seq

    w_out, mem_w_kv, a_w_in, w_kv, b_w_in = (
        w.astype(BF16) for w in (w_out, mem_w_kv, a_w_in, w_kv, b_w_in))
    ffn_w = (_chunk_major(ffn_w_gu[0, 0].astype(BF16), FFN_TF), ffn_w_down[0, 0].astype(BF16))

    def ffn(x, l, i, ffn_w):
        last = (l == depth - 1 and i == 1)
        nxt = None if last else (ffn_w_gu, ffn_w_down) + ((l, 1) if i == 0 else (l + 1, 0))
        res = _ffn(x, norm_g, l, 2 * i, ffn_w[0], ffn_w[1], nxt)
        return (res[0], None) if last else (res[0], (res[1], res[2]))

    norm_g = norm_g[:, :, None, :]
    mem_norm_g = mem_norm_g[:, None, :]
    mem_q_norm_g = mem_q_norm_g[:, None, :]
    mem_k_norm_g = mem_k_norm_g[:, None, :]
    a_v_norm_g = a_v_norm_g[:, None, :]
    b_q_norm_g = b_q_norm_g[:, None, :]
    b_subln_g = b_subln_g[:, None, :]
    a_b_s_t = jnp.swapaxes(a_b_s, 1, 2)

    pos = jnp.arange(seq, dtype=F32)
    inv = ROPE_THETA ** (-jnp.arange(0, HEAD_DIM, 2, dtype=F32) / HEAD_DIM)
    ang = pos[:, None] * inv[None, :]
    ang = jnp.concatenate([ang, ang], axis=-1)
    cos = jnp.cos(ang)
    sign = jnp.where(jnp.arange(HEAD_DIM) < HEAD_DIM // 2, -1.0, 1.0).astype(F32)
    sin_signed = jnp.sin(ang) * sign[None, :]

    x = x.reshape(n, d)
    mk, mv = _mem_kv(mem.reshape(batch * mem.shape[1], d), mem_norm_g, mem_w_kv, mem_k_norm_g)

    k_sh = v_sh = None
    for l in range(depth):
        if l == n_a:
            k_sh, v_sh = _kv_proj(x, kv_norm_g[None, :], w_kv, k_norm_g[None, :], cos, sin_signed,
                                  batch, seq, mix_w, b_subln_g.shape[-1])
        x, ffn_w = ffn(x, l, 0, ffn_w)
        if l < n_a:
            mix, mo = _mixer_a(x, norm_g, l, a_w_in, a_v_norm_g, a_w_s, a_b_s_t, mem_q_norm_g,
                               mk, mv, seq)
        else:
            j = l - n_a
            q, mo = _proj_b(x, norm_g, l, b_w_in, j, b_q_norm_g, cos, sin_signed, mem_q_norm_g,
                            mk, mv, seq)
            lam_init = 0.8 - 0.6 * math.exp(-0.3 * l)
            mix = _diff_attn(q, k_sh, v_sh, b_lambda, b_subln_g, j, lam_init, batch, seq)
        x = _out_proj(x, mix, mo, w_out, l)
        x, ffn_w = ffn(x, l, 1, ffn_w)
    return x.reshape(batch, seq, d)
```

```python
import functools
import math

import jax
import jax.numpy as jnp
from jax import lax
from jax.experimental import pallas as pl
from jax.experimental.pallas import tpu as pltpu

EPS = 1e-6
HEAD_DIM = 128
MEM_HEADS = 4
MEM_W = MEM_HEADS * HEAD_DIM
CHUNK = 128
GMLP_GROUPS = 6
ROPE_THETA = 10000.0
ATTN_SCALE = HEAD_DIM ** -0.5
SQRT_HALF = 0.7071067811865476

V7X_VMEM_BYTES = 64 * 1024 * 1024
VMEM_LIMIT_BYTES = V7X_VMEM_BYTES - 8 * 1024 * 1024
FFN_VMEM_LIMIT_BYTES = V7X_VMEM_BYTES - 5 * 1024 * 1024
FFN_TM = 1024
FFN_TF = 512
PROJ_TM = 512
PROJ_B_TM = 256
PROJ_SUB = 256
PROJ_TN = 512
OUT_TM = 512
ATTN_T = 512
LOG2E = 1.4426950408889634
Q_PRESCALE = ATTN_SCALE * LOG2E

F32 = jnp.float32
BF16 = jnp.bfloat16
NT_DIMS = (((1,), (1,)), ((), ()))


def _params(*semantics):
    return pltpu.CompilerParams(dimension_semantics=semantics,
                                vmem_limit_bytes=VMEM_LIMIT_BYTES)


def _rms(x, g):
    return x * lax.rsqrt(jnp.mean(x * x, axis=-1, keepdims=True) + EPS) * g


def _gelu(z):
    return 0.5 * z * (1.0 + lax.erf(z * SQRT_HALF))


def _rope(r, cos, sin_signed):
    return r * cos + pltpu.roll(r, HEAD_DIM // 2, 1) * sin_signed


def _mem_attn(zq, gq, mk_ref, mv_ref, mo_ref, rows):
    for hh in range(MEM_HEADS):
        cols = slice(hh * HEAD_DIM, (hh + 1) * HEAD_DIM)
        qh = _rms(zq[:, cols], gq).astype(BF16)
        s = lax.dot_general(qh, mk_ref[:, cols], NT_DIMS, preferred_element_type=F32) * ATTN_SCALE
        e = jnp.exp(s - jnp.max(s, axis=-1, keepdims=True))
        p = e / jnp.sum(e, axis=-1, keepdims=True)
        mo_ref[rows, cols] = jnp.dot(p.astype(BF16), mv_ref[:, cols],
                                     preferred_element_type=F32).astype(BF16)


def _sub_blocks(tm):
    return [slice(s, s + PROJ_SUB) for s in range(0, tm, PROJ_SUB)]


def _ffn_kernel(x_ref, g_ref, wg_ref, wu_ref, wd_ref, *refs):
    n_cast = (len(refs) - 2) // 2
    cast_in, o_ref, cast_out, h_ref = refs[:n_cast], refs[n_cast], refs[n_cast + 1:-1], refs[-1]
    j = pl.program_id(1)

    def step(first):
        if first:
            h_ref[...] = _rms(x_ref[...], g_ref[...]).astype(BF16)
        h = h_ref[...]
        g = jnp.dot(h, wg_ref[...], preferred_element_type=F32)
        u = jnp.dot(h, wu_ref[...], preferred_element_type=F32)
        for src_ref, dst_ref in zip(cast_in, cast_out):
            dst_ref[...] = src_ref[...].astype(BF16)
        act = (0.5 * g * jax.nn.sigmoid(g)) * u
        down = jnp.dot(act.astype(BF16), wd_ref[...], preferred_element_type=F32)
        if first:
            o_ref[...] = x_ref[...] + down
        else:
            o_ref[...] += down

    pl.when(j == 0)(functools.partial(step, True))
    pl.when(j > 0)(functools.partial(step, False))


def _ffn(x, norm_g, l, slot, w_gu, w_down, next_w=None):
    n, d = x.shape
    f = w_down.shape[0]
    tm, tf = FFN_TM, FFN_TF
    nr, nf = n // tm, f // tf
    in_specs = [
        pl.BlockSpec((tm, d), lambda r, j: (r, 0)),
        pl.BlockSpec((None, None, 1, d), lambda r, j: (l, slot, 0, 0)),
        pl.BlockSpec((d, tf), lambda r, j: (0, j)),
        pl.BlockSpec((d, tf), lambda r, j: (0, j + nf)),
        pl.BlockSpec((tf, d), lambda r, j: (j, 0)),
    ]
    out_specs = [pl.BlockSpec((tm, d), lambda r, j: (r, 0))]
    out_shape = [jax.ShapeDtypeStruct((n, d), F32)]
    args = [x, norm_g, w_gu, w_gu, w_down]
    if next_w is not None:
        gu32, down32, l2, i2 = next_w
        steps = nr * nf
        assert d % nr == 0 and (2 * f) % nf == 0 and f % steps == 0
        in_specs += [
            pl.BlockSpec((None, None, d // nr, 2 * f // nf), lambda r, j: (l2, i2, r, j)),
            pl.BlockSpec((None, None, f // steps, d), lambda r, j: (l2, i2, r * nf + j, 0)),
        ]
        out_specs += [pl.BlockSpec((d // nr, 2 * f // nf), lambda r, j: (r, j)),
                      pl.BlockSpec((f // steps, d), lambda r, j: (r * nf + j, 0))]
        out_shape += [jax.ShapeDtypeStruct((d, 2 * f), BF16), jax.ShapeDtypeStruct((f, d), BF16)]
        args += [gu32, down32]
    return pl.pallas_call(
        _ffn_kernel,
        grid=(nr, nf),
        in_specs=in_specs,
        out_specs=out_specs,
        out_shape=out_shape,
        scratch_shapes=[pltpu.VMEM((tm, d), BF16)],
        compiler_params=pltpu.CompilerParams(dimension_semantics=("parallel", "arbitrary"),
                                             vmem_limit_bytes=FFN_VMEM_LIMIT_BYTES),
        name="ffn",
    )(*args)


def _mem_kv_kernel(mem_ref, g_ref, w_ref, gk_ref, mk_ref, mv_ref):
    h = _rms(mem_ref[...], g_ref[...]).astype(BF16)
    kv = jnp.dot(h, w_ref[...], preferred_element_type=F32)
    gk = gk_ref[...]
    for hh in range(MEM_HEADS):
        cols = slice(hh * HEAD_DIM, (hh + 1) * HEAD_DIM)
        mk_ref[:, cols] = _rms(kv[:, cols], gk).astype(BF16)
    mv_ref[...] = kv[:, MEM_W:].astype(BF16)


def _mem_kv(mem2, mem_norm_g, mem_w_kv, mem_k_norm_g):
    rows, d = mem2.shape
    depth = mem_w_kv.shape[0]
    out = jax.ShapeDtypeStruct((depth, rows, MEM_W), BF16)
    return pl.pallas_call(
        _mem_kv_kernel,
        grid=(depth,),
        in_specs=[
            pl.BlockSpec((rows, d), lambda l: (0, 0)),
            pl.BlockSpec((None, 1, d), lambda l: (l, 0, 0)),
            pl.BlockSpec((None, d, 2 * MEM_W), lambda l: (l, 0, 0)),
            pl.BlockSpec((None, 1, HEAD_DIM), lambda l: (l, 0, 0)),
        ],
        out_specs=[pl.BlockSpec((None, rows, MEM_W), lambda l: (l, 0, 0))] * 2,
        out_shape=[out, out],
        compiler_params=_params("parallel"),
        name="mem_kv",
    )(mem2, mem_norm_g, mem_w_kv, mem_k_norm_g)


def _mixer_a_kernel(x_ref, g_ref, w_ref, gv_ref, ws_ref, bs_ref, gq_ref, mk_ref, mv_ref,
                    mix_ref, mo_ref, h_ref, u_ref, v_ref):
    tm = x_ref.shape[0]
    mix_w = u_ref.shape[1]
    tn = PROJ_TN
    gw = mix_w // GMLP_GROUPS
    causal = (lax.broadcasted_iota(jnp.int32, (CHUNK, CHUNK), 0)
              >= lax.broadcasted_iota(jnp.int32, (CHUNK, CHUNK), 1))
    w_s = [jnp.where(causal, ws_ref[grp], 0.0).astype(BF16) for grp in range(GMLP_GROUPS)]
    h_ref[...] = _rms(x_ref[...], g_ref[...]).astype(BF16)
    for c in range(mix_w // tn):
        cols = slice(c * tn, (c + 1) * tn)
        u_ref[:, cols] = _gelu(jnp.dot(h_ref[...], w_ref[:, cols], preferred_element_type=F32))
        v_ref[:, cols] = _gelu(jnp.dot(h_ref[...], w_ref[:, mix_w + c * tn:mix_w + (c + 1) * tn],
                                       preferred_element_type=F32))
    zq = jnp.dot(h_ref[...], w_ref[:, 2 * mix_w:], preferred_element_type=F32)
    _mem_attn(zq, gq_ref[...], mk_ref, mv_ref, mo_ref, slice(0, tm))
    for t in range(0, tm, CHUNK):
        chunk = slice(t, t + CHUNK)
        vn = _rms(v_ref[chunk, :], gv_ref[...]).astype(BF16)
        for grp in range(GMLP_GROUPS):
            cols = slice(grp * gw, (grp + 1) * gw)
            mixed = (jnp.dot(w_s[grp], vn[:, cols], preferred_element_type=F32)
                     + bs_ref[:, grp:grp + 1])
            mix_ref[chunk, cols] = (u_ref[chunk, cols] * mixed).astype(BF16)


def _mixer_a(x, norm_g, l, w_in, v_norm_g, w_s, b_s_t, mem_q_norm_g, mk, mv, seq):
    n, d = x.shape
    mix_w = v_norm_g.shape[-1]
    tm = PROJ_TM
    blocks_per_seq = seq // tm
    mem_len = mk.shape[1] // (n // seq)
    mem_spec = pl.BlockSpec((None, mem_len, MEM_W), lambda r: (l, r // blocks_per_seq, 0))
    return pl.pallas_call(
        _mixer_a_kernel,
        grid=(n // tm,),
        in_specs=[
            pl.BlockSpec((tm, d), lambda r: (r, 0)),
            pl.BlockSpec((None, None, 1, d), lambda r: (l, 1, 0, 0)),
            pl.BlockSpec((None, d, 2 * mix_w + MEM_W), lambda r: (l, 0, 0)),
            pl.BlockSpec((None, 1, mix_w), lambda r: (l, 0, 0)),
            pl.BlockSpec((None, GMLP_GROUPS, CHUNK, CHUNK), lambda r: (l, 0, 0, 0)),
            pl.BlockSpec((None, CHUNK, GMLP_GROUPS), lambda r: (l, 0, 0)),
            pl.BlockSpec((None, 1, HEAD_DIM), lambda r: (l, 0, 0)),
            mem_spec, mem_spec,
        ],
        out_specs=[pl.BlockSpec((tm, mix_w), lambda r: (r, 0)),
                   pl.BlockSpec((tm, MEM_W), lambda r: (r, 0))],
        out_shape=[jax.ShapeDtypeStruct((n, mix_w), BF16),
                   jax.ShapeDtypeStruct((n, MEM_W), BF16)],
        scratch_shapes=[pltpu.VMEM((tm, d), BF16),
                        pltpu.VMEM((tm, mix_w), F32),
                        pltpu.VMEM((tm, mix_w), F32)],
        compiler_params=_params("parallel"),
        name="mixer_a",
    )(x, norm_g, w_in, v_norm_g, w_s, b_s_t, mem_q_norm_g, mk, mv)


def _kv_proj_kernel(x_ref, g_ref, w_ref, gk_ref, cos_ref, sin_ref, k_ref, vt_ref, h_ref, v_ref):
    qk_w = k_ref.shape[0] * HEAD_DIM
    heads, vd, tm = vt_ref.shape
    tn = PROJ_TN
    gk = gk_ref[...]
    for rows in _sub_blocks(tm):
        h_ref[rows, :] = _rms(x_ref[rows, :], g_ref[...]).astype(BF16)
    for rows in _sub_blocks(tm):
        cos, sin = cos_ref[rows, :], sin_ref[rows, :]
        for c in range(qk_w // tn):
            z = jnp.dot(h_ref[rows, :], w_ref[:, c * tn:(c + 1) * tn], preferred_element_type=F32)
            for k in range(tn // HEAD_DIM):
                r = _rms(z[:, k * HEAD_DIM:(k + 1) * HEAD_DIM], gk)
                k_ref[c * (tn // HEAD_DIM) + k, rows, :] = _rope(r, cos, sin).astype(BF16)
        for hh in range(heads):
            v_ref[rows, :] = jnp.dot(h_ref[rows, :], w_ref[:, qk_w + hh * vd:qk_w + (hh + 1) * vd],
                                     preferred_element_type=F32)
            vt_ref[hh, :, rows] = v_ref[rows, :].T.astype(BF16)


def _kv_proj(x, kv_norm_g, w_kv, k_norm_g, cos, sin_signed, batch, seq, qk_w, vd):
    n, d = x.shape
    heads = (w_kv.shape[1] - qk_w) // vd
    tm = ATTN_T
    blocks_per_seq = seq // tm
    rope_spec = pl.BlockSpec((tm, HEAD_DIM), lambda r: (r % blocks_per_seq, 0))
    return pl.pallas_call(
        _kv_proj_kernel,
        grid=(n // tm,),
        in_specs=[
            pl.BlockSpec((tm, d), lambda r: (r, 0)),
            pl.BlockSpec((1, d), lambda r: (0, 0)),
            pl.BlockSpec((d, qk_w + heads * vd), lambda r: (0, 0)),
            pl.BlockSpec((1, HEAD_DIM), lambda r: (0, 0)),
            rope_spec, rope_spec,
        ],
        out_specs=[pl.BlockSpec((None, qk_w // HEAD_DIM, tm, HEAD_DIM),
                                lambda r: (r // blocks_per_seq, 0, r % blocks_per_seq, 0)),
                   pl.BlockSpec((None, None, heads, vd, tm),
                                lambda r: (r // blocks_per_seq, r % blocks_per_seq, 0, 0, 0))],
        out_shape=[jax.ShapeDtypeStruct((batch, qk_w // HEAD_DIM, seq, HEAD_DIM), BF16),
                   jax.ShapeDtypeStruct((batch, blocks_per_seq, heads, vd, tm), BF16)],
        scratch_shapes=[pltpu.VMEM((tm, d), BF16), pltpu.VMEM((tm, vd), F32)],
        compiler_params=_params("parallel"),
        name="kv_proj",
    )(x, kv_norm_g, w_kv, k_norm_g, cos, sin_signed)


def _proj_b_kernel(x_ref, g_ref, w_ref, gqn_ref, cos_ref, sin_ref, gq_ref, mk_ref, mv_ref,
                   q_ref, mo_ref, h_ref):
    n_qk, tm, _ = q_ref.shape
    qk_w = n_qk * HEAD_DIM
    tn = PROJ_TN
    gqn = gqn_ref[...]
    for rows in _sub_blocks(tm):
        h_ref[rows, :] = _rms(x_ref[rows, :], g_ref[...]).astype(BF16)
    for rows in _sub_blocks(tm):
        cos, sin = cos_ref[rows, :], sin_ref[rows, :]
        for c in range(qk_w // tn):
            z = jnp.dot(h_ref[rows, :], w_ref[:, c * tn:(c + 1) * tn], preferred_element_type=F32)
            for k in range(tn // HEAD_DIM):
                r = _rms(z[:, k * HEAD_DIM:(k + 1) * HEAD_DIM], gqn)
                q_ref[c * (tn // HEAD_DIM) + k, rows, :] = (
                    _rope(r, cos, sin) * Q_PRESCALE).astype(BF16)
        zq = jnp.dot(h_ref[rows, :], w_ref[:, qk_w:], preferred_element_type=F32)
        _mem_attn(zq, gq_ref[...], mk_ref, mv_ref, mo_ref, rows)


def _proj_b(x, norm_g, l, w_in, j, q_norm_g, cos, sin_signed, mem_q_norm_g, mk, mv, seq):
    n, d = x.shape
    qk_w = w_in.shape[2] - MEM_W
    tm = PROJ_B_TM
    blocks_per_seq = seq // tm
    mem_len = mk.shape[1] // (n // seq)
    rope_spec = pl.BlockSpec((tm, HEAD_DIM), lambda r: (r % blocks_per_seq, 0))
    mem_spec = pl.BlockSpec((None, mem_len, MEM_W), lambda r: (l, r // blocks_per_seq, 0))
    return pl.pallas_call(
        _proj_b_kernel,
        grid=(n // tm,),
        in_specs=[
            pl.BlockSpec((tm, d), lambda r: (r, 0)),
            pl.BlockSpec((None, None, 1, d), lambda r: (l, 1, 0, 0)),
            pl.BlockSpec((None, d, qk_w + MEM_W), lambda r: (j, 0, 0)),
            pl.BlockSpec((None, 1, HEAD_DIM), lambda r: (j, 0, 0)),
            rope_spec, rope_spec,
            pl.BlockSpec((None, 1, HEAD_DIM), lambda r: (l, 0, 0)),
            mem_spec, mem_spec,
        ],
        out_specs=[pl.BlockSpec((None, qk_w // HEAD_DIM, tm, HEAD_DIM),
                                lambda r: (r // blocks_per_seq, 0, r % blocks_per_seq, 0)),
                   pl.BlockSpec((tm, MEM_W), lambda r: (r, 0))],
        out_shape=[jax.ShapeDtypeStruct((n // seq, qk_w // HEAD_DIM, seq, HEAD_DIM), BF16),
                   jax.ShapeDtypeStruct((n, MEM_W), BF16)],
        scratch_shapes=[pltpu.VMEM((tm, d), BF16)],
        compiler_params=_params("parallel"),
        name="proj_b",
    )(x, norm_g, w_in, q_norm_g, cos, sin_signed, mem_q_norm_g, mk, mv)


def _diff_attn_kernel(lam_ref, gs_ref, q0_ref, q1_ref, k0_ref, k1_ref, vt_ref, o_ref, acc_ref,
                      *, lam_init):
    t = ATTN_T
    qi = pl.program_id(2)
    q_refs, k_refs = (q0_ref, q1_ref), (k0_ref, k1_ref)
    acc_ref[...] = jnp.zeros(acc_ref.shape, F32)

    def scores(ki, c):
        rows = pl.ds(pl.multiple_of(ki * t, t), t)
        return lax.dot_general(k_refs[c][rows, :], q_refs[c][...], NT_DIMS,
                               preferred_element_type=F32)

    def update(ki, c, st, stat, masked):
        m_old, l_old = stat
        if masked:
            key_le_query = (lax.broadcasted_iota(jnp.int32, (t, t), 0)
                            <= lax.broadcasted_iota(jnp.int32, (t, t), 1))
            st = jnp.where(key_le_query, st, -jnp.inf)
        m_new = jnp.maximum(m_old, jnp.max(st, axis=0, keepdims=True))
        alpha = jnp.exp2(m_old - m_new)
        pt = jnp.exp2(st - m_new)
        acc_ref[c] = alpha * acc_ref[c] + jnp.dot(vt_ref[ki], pt.astype(BF16),
                                                  preferred_element_type=F32)
        return m_new, alpha * l_old + jnp.sum(pt, axis=0, keepdims=True)

    def blocks(kis, stats, masked_last):
        stats = list(stats)
        work = [(ki, c, masked_last and n == len(kis) - 1)
                for n, ki in enumerate(kis) for c in range(2)]
        pending = [scores(ki, c) for ki, c, _ in work[:2]]
        for n, (ki, c, masked) in enumerate(work):
            if n + 2 < len(work):
                pending.append(scores(work[n + 2][0], work[n + 2][1]))
            stats[c] = update(ki, c, pending[n], stats[c], masked)
        return tuple(stats)

    init = ((jnp.full((1, t), -jnp.inf, F32), jnp.zeros((1, t), F32)),) * 2
    stats = lax.fori_loop(0, qi // 2, lambda j, s: blocks((2 * j, 2 * j + 1), s, False), init)
    stats = lax.cond(qi % 2 == 1,
                     lambda s: blocks((qi - 1, qi), s, True),
                     lambda s: blocks((qi,), s, True), stats)
    l0, l1 = stats[0][1], stats[1][1]

    lp = lam_ref[...]
    lam = (jnp.exp(jnp.sum(lp[0:1] * lp[1:2], axis=-1, keepdims=True))
           - jnp.exp(jnp.sum(lp[2:3] * lp[3:4], axis=-1, keepdims=True)) + lam_init)
    ot = acc_ref[0] / l0 - lam * (acc_ref[1] / l1)
    ot = ot * lax.rsqrt(jnp.mean(ot * ot, axis=0, keepdims=True) + EPS)
    o_ref[...] = (ot.T * gs_ref[...] * (1.0 - lam_init)).astype(BF16)


def _diff_attn(q, k, vt, b_lambda, subln_g, j, lam_init, batch, seq):
    heads = q.shape[1] // 2
    _, nk, _, vd, t = vt.shape
    n = batch * seq
    o = pl.pallas_call(
        functools.partial(_diff_attn_kernel, lam_init=lam_init),
        grid=(batch, heads, seq // t),
        in_specs=[
            pl.BlockSpec((None, 4, HEAD_DIM), lambda b, h, i: (j, 0, 0)),
            pl.BlockSpec((None, 1, vd), lambda b, h, i: (j, 0, 0)),
            pl.BlockSpec((None, None, t, HEAD_DIM), lambda b, h, i: (b, h, i, 0)),
            pl.BlockSpec((None, None, t, HEAD_DIM), lambda b, h, i: (b, heads + h, i, 0)),
            pl.BlockSpec((None, None, seq, HEAD_DIM), lambda b, h, i: (b, h, 0, 0)),
            pl.BlockSpec((None, None, seq, HEAD_DIM), lambda b, h, i: (b, heads + h, 0, 0)),
            pl.BlockSpec((None, nk, None, vd, t), lambda b, h, i: (b, 0, h, 0, 0)),
        ],
        out_specs=pl.BlockSpec((None, t, vd), lambda b, h, i: (b, i, h)),
        out_shape=jax.ShapeDtypeStruct((batch, seq, heads * vd), BF16),
        scratch_shapes=[pltpu.VMEM((2, vd, t), F32)],
        compiler_params=_params("parallel", "parallel", "parallel"),
        name="diff_attn",
    )(b_lambda, subln_g, q, q, k, k, vt)
    return o.reshape(n, heads * vd)


def _out_proj_kernel(x_ref, mix_ref, mo_ref, w_ref, o_ref):
    mix_w = mix_ref.shape[1]
    tn = PROJ_TN
    for c in range(o_ref.shape[1] // tn):
        cols = slice(c * tn, (c + 1) * tn)
        y = (jnp.dot(mix_ref[...], w_ref[:mix_w, cols], preferred_element_type=F32)
             + jnp.dot(mo_ref[...], w_ref[mix_w:, cols], preferred_element_type=F32))
        o_ref[:, cols] = x_ref[:, cols] + y


def _out_proj(x, mix, mo, w_out, l):
    n, d = x.shape
    mix_w = mix.shape[1]
    tm = OUT_TM
    return pl.pallas_call(
        _out_proj_kernel,
        grid=(n // tm,),
        in_specs=[
            pl.BlockSpec((tm, d), lambda r: (r, 0)),
            pl.BlockSpec((tm, mix_w), lambda r: (r, 0)),
            pl.BlockSpec((tm, MEM_W), lambda r: (r, 0)),
            pl.BlockSpec((None, d, d), lambda r: (l, 0, 0)),
        ],
        out_specs=pl.BlockSpec((tm, d), lambda r: (r, 0)),
        out_shape=jax.ShapeDtypeStruct((n, d), F32),
        compiler_params=_params("parallel"),
        name="out_proj",
    )(x, mix, mo, w_out)


def kernel(x, mem, norm_g, ffn_w_gu, ffn_w_down, w_out, mem_norm_g, mem_w_kv, mem_q_norm_g,
           mem_k_norm_g, a_w_in, a_v_norm_g, a_w_s, a_b_s, kv_norm_g, w_kv, k_norm_g, b_w_in,
           b_q_norm_g, b_lambda, b_subln_g):
    batch, seq, d = x.shape
    depth = norm_g.shape[0]
    n_a = a_w_in.shape[0]
    mix_w = d - MEM_W
    n = batch * seq

    w_out, mem_w_kv, a_w_in, w_kv, b_w_in = (
        w.astype(BF16) for w in (w_out, mem_w_kv, a_w_in, w_kv, b_w_in))
    ffn_w = (ffn_w_gu[0, 0].astype(BF16), ffn_w_down[0, 0].astype(BF16))

    def ffn(x, l, i, ffn_w):
        last = (l == depth - 1 and i == 1)
        nxt = None if last else (ffn_w_gu, ffn_w_down) + ((l, 1) if i == 0 else (l + 1, 0))
        res = _ffn(x, norm_g, l, 2 * i, ffn_w[0], ffn_w[1], nxt)
        return (res[0], None) if last else (res[0], (res[1], res[2]))

    norm_g = norm_g[:, :, None, :]
    mem_norm_g = mem_norm_g[:, None, :]
    mem_q_norm_g = mem_q_norm_g[:, None, :]
    mem_k_norm_g = mem_k_norm_g[:, None, :]
    a_v_norm_g = a_v_norm_g[:, None, :]
    b_q_norm_g = b_q_norm_g[:, None, :]
    b_subln_g = b_subln_g[:, None, :]
    a_b_s_t = jnp.swapaxes(a_b_s, 1, 2)

    pos = jnp.arange(seq, dtype=F32)
    inv = ROPE_THETA ** (-jnp.arange(0, HEAD_DIM, 2, dtype=F32) / HEAD_DIM)
    ang = pos[:, None] * inv[None, :]
    ang = jnp.concatenate([ang, ang], axis=-1)
    cos = jnp.cos(ang)
    sign = jnp.where(jnp.arange(HEAD_DIM) < HEAD_DIM // 2, -1.0, 1.0).astype(F32)
    sin_signed = jnp.sin(ang) * sign[None, :]

    x = x.reshape(n, d)
    mk, mv = _mem_kv(mem.reshape(batch * mem.shape[1], d), mem_norm_g, mem_w_kv, mem_k_norm_g)

    k_sh = v_sh = None
    for l in range(depth):
        if l == n_a:
            k_sh, v_sh = _kv_proj(x, kv_norm_g[None, :], w_kv, k_norm_g[None, :], cos, sin_signed,
                                  batch, seq, mix_w, b_subln_g.shape[-1])
        x, ffn_w = ffn(x, l, 0, ffn_w)
        if l < n_a:
            mix, mo = _mixer_a(x, norm_g, l, a_w_in, a_v_norm_g, a_w_s, a_b_s_t, mem_q_norm_g,
                               mk, mv, seq)
        else:
            j = l - n_a
            q, mo = _proj_b(x, norm_g, l, b_w_in, j, b_q_norm_g, cos, sin_signed, mem_q_norm_g,
                            mk, mv, seq)
            lam_init = 0.8 - 0.6 * math.exp(-0.3 * l)
            mix = _diff_attn(q, k_sh, v_sh, b_lambda, b_subln_g, j, lam_init, batch, seq)
        x = _out_proj(x, mix, mo, w_out, l)
        x, ffn_w = ffn(x, l, 1, ffn_w)
    return x.reshape(batch, seq, d)
```

```python
import functools
import math

import jax
import jax.numpy as jnp
from jax import lax
from jax.experimental import pallas as pl
from jax.experimental.pallas import tpu as pltpu

EPS = 1e-6
HEAD_DIM = 128
MEM_HEADS = 4
MEM_W = MEM_HEADS * HEAD_DIM
CHUNK = 128
GMLP_GROUPS = 6
ROPE_THETA = 10000.0
ATTN_SCALE = HEAD_DIM ** -0.5
SQRT_HALF = 0.7071067811865476

V7X_VMEM_BYTES = 64 * 1024 * 1024
VMEM_LIMIT_BYTES = V7X_VMEM_BYTES - 8 * 1024 * 1024
FFN_VMEM_LIMIT_BYTES = V7X_VMEM_BYTES - 5 * 1024 * 1024
FFN_TM = 1024
FFN_TF = 512
PROJ_TM = 512
PROJ_B_TM = 256
PROJ_SUB = 256
PROJ_TN = 512
OUT_TM = 512
ATTN_T = 512
LOG2E = 1.4426950408889634
Q_PRESCALE = ATTN_SCALE * LOG2E

F32 = jnp.float32
BF16 = jnp.bfloat16
NT_DIMS = (((1,), (1,)), ((), ()))


def _params(*semantics):
    return pltpu.CompilerParams(dimension_semantics=semantics,
                                vmem_limit_bytes=VMEM_LIMIT_BYTES)


def _rms(x, g):
    return x * lax.rsqrt(jnp.mean(x * x, axis=-1, keepdims=True) + EPS) * g


def _gelu(z):
    return 0.5 * z * (1.0 + lax.erf(z * SQRT_HALF))


def _rope(r, cos, sin_signed):
    return r * cos + pltpu.roll(r, HEAD_DIM // 2, 1) * sin_signed


def _mem_attn(zq, gq, mk_ref, mv_ref, mo_ref, rows):
    for hh in range(MEM_HEADS):
        cols = slice(hh * HEAD_DIM, (hh + 1) * HEAD_DIM)
        qh = _rms(zq[:, cols], gq).astype(BF16)
        s = lax.dot_general(qh, mk_ref[:, cols], NT_DIMS, preferred_element_type=F32) * ATTN_SCALE
        e = jnp.exp(s - jnp.max(s, axis=-1, keepdims=True))
        p = e / jnp.sum(e, axis=-1, keepdims=True)
        mo_ref[rows, cols] = jnp.dot(p.astype(BF16), mv_ref[:, cols],
                                     preferred_element_type=F32).astype(BF16)


def _sub_blocks(tm):
    return [slice(s, s + PROJ_SUB) for s in range(0, tm, PROJ_SUB)]


def _ffn_kernel(x_ref, g_ref, wg_ref, wu_ref, wd_ref, *refs):
    n_cast = (len(refs) - 2) // 2
    cast_in, o_ref, cast_out, h_ref = refs[:n_cast], refs[n_cast], refs[n_cast + 1:-1], refs[-1]
    j = pl.program_id(1)

    def step(first):
        if first:
            h_ref[...] = _rms(x_ref[...], g_ref[...]).astype(BF16)
        h = h_ref[...]
        g = jnp.dot(h, wg_ref[...], preferred_element_type=F32)
        u = jnp.dot(h, wu_ref[...], preferred_element_type=F32)
        for src_ref, dst_ref in zip(cast_in, cast_out):
            dst_ref[...] = src_ref[...].astype(BF16)
        act = (0.5 * g * jax.nn.sigmoid(g)) * u
        down = jnp.dot(act.astype(BF16), wd_ref[...], preferred_element_type=F32)
        if first:
            o_ref[...] = x_ref[...] + down
        else:
            o_ref[...] += down

    pl.when(j == 0)(functools.partial(step, True))
    pl.when(j > 0)(functools.partial(step, False))


def _ffn(x, norm_g, l, slot, w_gu, w_down, next_w=None):
    n, d = x.shape
    f = w_down.shape[0]
    tm, tf = FFN_TM, FFN_TF
    nr, nf = n // tm, f // tf
    in_specs = [
        pl.BlockSpec((tm, d), lambda r, j: (r, 0)),
        pl.BlockSpec((None, None, 1, d), lambda r, j: (l, slot, 0, 0)),
        pl.BlockSpec((d, tf), lambda r, j: (0, j)),
        pl.BlockSpec((d, tf), lambda r, j: (0, j + nf)),
        pl.BlockSpec((tf, d), lambda r, j: (j, 0)),
    ]
    out_specs = [pl.BlockSpec((tm, d), lambda r, j: (r, 0))]
    out_shape = [jax.ShapeDtypeStruct((n, d), F32)]
    args = [x, norm_g, w_gu, w_gu, w_down]
    if next_w is not None:
        gu32, down32, l2, i2 = next_w
        steps = nr * nf
        assert d % nr == 0 and (2 * f) % nf == 0 and f % steps == 0
        in_specs += [
            pl.BlockSpec((None, None, d // nr, 2 * f // nf), lambda r, j: (l2, i2, r, j)),
            pl.BlockSpec((None, None, f // steps, d), lambda r, j: (l2, i2, r * nf + j, 0)),
        ]
        out_specs += [pl.BlockSpec((d // nr, 2 * f // nf), lambda r, j: (r, j)),
                      pl.BlockSpec((f // steps, d), lambda r, j: (r * nf + j, 0))]
        out_shape += [jax.ShapeDtypeStruct((d, 2 * f), BF16), jax.ShapeDtypeStruct((f, d), BF16)]
        args += [gu32, down32]
    return pl.pallas_call(
        _ffn_kernel,
        grid=(nr, nf),
        in_specs=in_specs,
        out_specs=out_specs,
        out_shape=out_shape,
        scratch_shapes=[pltpu.VMEM((tm, d), BF16)],
        compiler_params=pltpu.CompilerParams(dimension_semantics=("parallel", "arbitrary"),
                                             vmem_limit_bytes=FFN_VMEM_LIMIT_BYTES),
        name="ffn",
    )(*args)


def _mem_kv_kernel(mem_ref, g_ref, w_ref, gk_ref, mk_ref, mv_ref):
    h = _rms(mem_ref[...], g_ref[...]).astype(BF16)
    kv = jnp.dot(h, w_ref[...], preferred_element_type=F32)
    gk = gk_ref[...]
    for hh in range(MEM_HEADS):
        cols = slice(hh * HEAD_DIM, (hh + 1) * HEAD_DIM)
        mk_ref[:, cols] = _rms(kv[:, cols], gk).astype(BF16)
    mv_ref[...] = kv[:, MEM_W:].astype(BF16)


def _mem_kv(mem2, mem_norm_g, mem_w_kv, mem_k_norm_g):
    rows, d = mem2.shape
    depth = mem_w_kv.shape[0]
    out = jax.ShapeDtypeStruct((depth, rows, MEM_W), BF16)
    return pl.pallas_call(
        _mem_kv_kernel,
        grid=(depth,),
        in_specs=[
            pl.BlockSpec((rows, d), lambda l: (0, 0)),
            pl.BlockSpec((None, 1, d), lambda l: (l, 0, 0)),
            pl.BlockSpec((None, d, 2 * MEM_W), lambda l: (l, 0, 0)),
            pl.BlockSpec((None, 1, HEAD_DIM), lambda l: (l, 0, 0)),
        ],
        out_specs=[pl.BlockSpec((None, rows, MEM_W), lambda l: (l, 0, 0))] * 2,
        out_shape=[out, out],
        compiler_params=_params("parallel"),
        name="mem_kv",
    )(mem2, mem_norm_g, mem_w_kv, mem_k_norm_g)


def _mixer_a_kernel(x_ref, g_ref, w_ref, gv_ref, ws_ref, bs_ref, gq_ref, mk_ref, mv_ref,
                    mix_ref, mo_ref, h_ref, u_ref, v_ref):
    tm = x_ref.shape[0]
    mix_w = u_ref.shape[1]
    tn = PROJ_TN
    gw = mix_w // GMLP_GROUPS
    causal = (lax.broadcasted_iota(jnp.int32, (CHUNK, CHUNK), 0)
              >= lax.broadcasted_iota(jnp.int32, (CHUNK, CHUNK), 1))
    w_s = [jnp.where(causal, ws_ref[grp], 0.0).astype(BF16) for grp in range(GMLP_GROUPS)]
    h_ref[...] = _rms(x_ref[...], g_ref[...]).astype(BF16)
    for c in range(mix_w // tn):
        cols = slice(c * tn, (c + 1) * tn)
        u_ref[:, cols] = _gelu(jnp.dot(h_ref[...], w_ref[:, cols], preferred_element_type=F32))
        v_ref[:, cols] = _gelu(jnp.dot(h_ref[...], w_ref[:, mix_w + c * tn:mix_w + (c + 1) * tn],
                                       preferred_element_type=F32))
    zq = jnp.dot(h_ref[...], w_ref[:, 2 * mix_w:], preferred_element_type=F32)
    _mem_attn(zq, gq_ref[...], mk_ref, mv_ref, mo_ref, slice(0, tm))
    for t in range(0, tm, CHUNK):
        chunk = slice(t, t + CHUNK)
        vn = _rms(v_ref[chunk, :], gv_ref[...]).astype(BF16)
        for grp in range(GMLP_GROUPS):
            cols = slice(grp * gw, (grp + 1) * gw)
            mixed = (jnp.dot(w_s[grp], vn[:, cols], preferred_element_type=F32)
                     + bs_ref[:, grp:grp + 1])
            mix_ref[chunk, cols] = (u_ref[chunk, cols] * mixed).astype(BF16)


def _mixer_a(x, norm_g, l, w_in, v_norm_g, w_s, b_s_t, mem_q_norm_g, mk, mv, seq):
    n, d = x.shape
    mix_w = v_norm_g.shape[-1]
    tm = PROJ_TM
    blocks_per_seq = seq // tm
    mem_len = mk.shape[1] // (n // seq)
    mem_spec = pl.BlockSpec((None, mem_len, MEM_W), lambda r: (l, r // blocks_per_seq, 0))
    return pl.pallas_call(
        _mixer_a_kernel,
        grid=(n // tm,),
        in_specs=[
            pl.BlockSpec((tm, d), lambda r: (r, 0)),
            pl.BlockSpec((None, None, 1, d), lambda r: (l, 1, 0, 0)),
            pl.BlockSpec((None, d, 2 * mix_w + MEM_W), lambda r: (l, 0, 0)),
            pl.BlockSpec((None, 1, mix_w), lambda r: (l, 0, 0)),
            pl.BlockSpec((None, GMLP_GROUPS, CHUNK, CHUNK), lambda r: (l, 0, 0, 0)),
            pl.BlockSpec((None, CHUNK, GMLP_GROUPS), lambda r: (l, 0, 0)),
            pl.BlockSpec((None, 1, HEAD_DIM), lambda r: (l, 0, 0)),
            mem_spec, mem_spec,
        ],
        out_specs=[pl.BlockSpec((tm, mix_w), lambda r: (r, 0)),
                   pl.BlockSpec((tm, MEM_W), lambda r: (r, 0))],
        out_shape=[jax.ShapeDtypeStruct((n, mix_w), BF16),
                   jax.ShapeDtypeStruct((n, MEM_W), BF16)],
        scratch_shapes=[pltpu.VMEM((tm, d), BF16),
                        pltpu.VMEM((tm, mix_w), F32),
                        pltpu.VMEM((tm, mix_w), F32)],
        compiler_params=_params("parallel"),
        name="mixer_a",
    )(x, norm_g, w_in, v_norm_g, w_s, b_s_t, mem_q_norm_g, mk, mv)


def _kv_proj_kernel(x_ref, g_ref, w_ref, gk_ref, cos_ref, sin_ref, k_ref, vt_ref, h_ref, v_ref):
    qk_w = k_ref.shape[0] * HEAD_DIM
    heads, vd, tm = vt_ref.shape
    tn = PROJ_TN
    gk = gk_ref[...]
    for rows in _sub_blocks(tm):
        h_ref[rows, :] = _rms(x_ref[rows, :], g_ref[...]).astype(BF16)
    for rows in _sub_blocks(tm):
        cos, sin = cos_ref[rows, :], sin_ref[rows, :]
        for c in range(qk_w // tn):
            z = jnp.dot(h_ref[rows, :], w_ref[:, c * tn:(c + 1) * tn], preferred_element_type=F32)
            for k in range(tn // HEAD_DIM):
                r = _rms(z[:, k * HEAD_DIM:(k + 1) * HEAD_DIM], gk)
                k_ref[c * (tn // HEAD_DIM) + k, rows, :] = _rope(r, cos, sin).astype(BF16)
        for hh in range(heads):
            v_ref[rows, :] = jnp.dot(h_ref[rows, :], w_ref[:, qk_w + hh * vd:qk_w + (hh + 1) * vd],
                                     preferred_element_type=F32)
            vt_ref[hh, :, rows] = v_ref[rows, :].T.astype(BF16)


def _kv_proj(x, kv_norm_g, w_kv, k_norm_g, cos, sin_signed, batch, seq, qk_w, vd):
    n, d = x.shape
    heads = (w_kv.shape[1] - qk_w) // vd
    tm = ATTN_T
    blocks_per_seq = seq // tm
    rope_spec = pl.BlockSpec((tm, HEAD_DIM), lambda r: (r % blocks_per_seq, 0))
    return pl.pallas_call(
        _kv_proj_kernel,
        grid=(n // tm,),
        in_specs=[
            pl.BlockSpec((tm, d), lambda r: (r, 0)),
            pl.BlockSpec((1, d), lambda r: (0, 0)),
            pl.BlockSpec((d, qk_w + heads * vd), lambda r: (0, 0)),
            pl.BlockSpec((1, HEAD_DIM), lambda r: (0, 0)),
            rope_spec, rope_spec,
        ],
        out_specs=[pl.BlockSpec((None, qk_w // HEAD_DIM, tm, HEAD_DIM),
                                lambda r: (r // blocks_per_seq, 0, r % blocks_per_seq, 0)),
                   pl.BlockSpec((None, None, heads, vd, tm),
                                lambda r: (r // blocks_per_seq, r % blocks_per_seq, 0, 0, 0))],
        out_shape=[jax.ShapeDtypeStruct((batch, qk_w // HEAD_DIM, seq, HEAD_DIM), BF16),
                   jax.ShapeDtypeStruct((batch, blocks_per_seq, heads, vd, tm), BF16)],
        scratch_shapes=[pltpu.VMEM((tm, d), BF16), pltpu.VMEM((tm, vd), F32)],
        compiler_params=_params("parallel"),
        name="kv_proj",
    )(x, kv_norm_g, w_kv, k_norm_g, cos, sin_signed)


def _proj_b_kernel(x_ref, g_ref, w_ref, gqn_ref, cos_ref, sin_ref, gq_ref, mk_ref, mv_ref,
                   q_ref, mo_ref, h_ref):
    n_qk, tm, _ = q_ref.shape
    qk_w = n_qk * HEAD_DIM
    tn = PROJ_TN
    gqn = gqn_ref[...]
    for rows in _sub_blocks(tm):
        h_ref[rows, :] = _rms(x_ref[rows, :], g_ref[...]).astype(BF16)
    for rows in _sub_blocks(tm):
        cos, sin = cos_ref[rows, :], sin_ref[rows, :]
        for c in range(qk_w // tn):
            z = jnp.dot(h_ref[rows, :], w_ref[:, c * tn:(c + 1) * tn], preferred_element_type=F32)
            for k in range(tn // HEAD_DIM):
                r = _rms(z[:, k * HEAD_DIM:(k + 1) * HEAD_DIM], gqn)
                q_ref[c * (tn // HEAD_DIM) + k, rows, :] = (
                    _rope(r, cos, sin) * Q_PRESCALE).astype(BF16)
        zq = jnp.dot(h_ref[rows, :], w_ref[:, qk_w:], preferred_element_type=F32)
        _mem_attn(zq, gq_ref[...], mk_ref, mv_ref, mo_ref, rows)


def _proj_b(x, norm_g, l, w_in, j, q_norm_g, cos, sin_signed, mem_q_norm_g, mk, mv, seq):
    n, d = x.shape
    qk_w = w_in.shape[2] - MEM_W
    tm = PROJ_B_TM
    blocks_per_seq = seq // tm
    mem_len = mk.shape[1] // (n // seq)
    rope_spec = pl.BlockSpec((tm, HEAD_DIM), lambda r: (r % blocks_per_seq, 0))
    mem_spec = pl.BlockSpec((None, mem_len, MEM_W), lambda r: (l, r // blocks_per_seq, 0))
    return pl.pallas_call(
        _proj_b_kernel,
        grid=(n // tm,),
        in_specs=[
            pl.BlockSpec((tm, d), lambda r: (r, 0)),
            pl.BlockSpec((None, None, 1, d), lambda r: (l, 1, 0, 0)),
            pl.BlockSpec((None, d, qk_w + MEM_W), lambda r: (j, 0, 0)),
            pl.BlockSpec((None, 1, HEAD_DIM), lambda r: (j, 0, 0)),
            rope_spec, rope_spec,
            pl.BlockSpec((None, 1, HEAD_DIM), lambda r: (l, 0, 0)),
            mem_spec, mem_spec,
        ],
        out_specs=[pl.BlockSpec((None, qk_w // HEAD_DIM, tm, HEAD_DIM),
                                lambda r: (r // blocks_per_seq, 0, r % blocks_per_seq, 0)),
                   pl.BlockSpec((tm, MEM_W), lambda r: (r, 0))],
        out_shape=[jax.ShapeDtypeStruct((n // seq, qk_w // HEAD_DIM, seq, HEAD_DIM), BF16),
                   jax.ShapeDtypeStruct((n, MEM_W), BF16)],
        scratch_shapes=[pltpu.VMEM((tm, d), BF16)],
        compiler_params=_params("parallel"),
        name="proj_b",
    )(x, norm_g, w_in, q_norm_g, cos, sin_signed, mem_q_norm_g, mk, mv)


def _diff_attn_kernel(lam_ref, gs_ref, q0a_ref, q1a_ref, q0b_ref, q1b_ref, k0_ref, k1_ref, vt_ref,
                      o_ref, acc_ref, *, lam_init):
    t = ATTN_T
    k_refs = (k0_ref, k1_ref)
    lp = lam_ref[...]
    lam = (jnp.exp(jnp.sum(lp[0:1] * lp[1:2], axis=-1, keepdims=True))
           - jnp.exp(jnp.sum(lp[2:3] * lp[3:4], axis=-1, keepdims=True)) + lam_init)

    def query_block(qi, q_refs, out_slot):
        acc_ref[...] = jnp.zeros(acc_ref.shape, F32)

        def scores(ki, c):
            rows = pl.ds(pl.multiple_of(ki * t, t), t)
            return lax.dot_general(k_refs[c][rows, :], q_refs[c][...], NT_DIMS,
                                   preferred_element_type=F32)

        def update(ki, c, st, stat, masked):
            m_old, l_old = stat
            if masked:
                key_le_query = (lax.broadcasted_iota(jnp.int32, (t, t), 0)
                                <= lax.broadcasted_iota(jnp.int32, (t, t), 1))
                st = jnp.where(key_le_query, st, -jnp.inf)
            m_new = jnp.maximum(m_old, jnp.max(st, axis=0, keepdims=True))
            alpha = jnp.exp2(m_old - m_new)
            pt = jnp.exp2(st - m_new)
            acc_ref[c] = alpha * acc_ref[c] + jnp.dot(vt_ref[ki], pt.astype(BF16),
                                                      preferred_element_type=F32)
            return m_new, alpha * l_old + jnp.sum(pt, axis=0, keepdims=True)

        def blocks(kis, stats, masked_last):
            stats = list(stats)
            work = [(ki, c, masked_last and n == len(kis) - 1)
                    for n, ki in enumerate(kis) for c in range(2)]
            pending = [scores(ki, c) for ki, c, _ in work[:2]]
            for n, (ki, c, masked) in enumerate(work):
                if n + 2 < len(work):
                    pending.append(scores(work[n + 2][0], work[n + 2][1]))
                stats[c] = update(ki, c, pending[n], stats[c], masked)
            return tuple(stats)

        init = ((jnp.full((1, t), -jnp.inf, F32), jnp.zeros((1, t), F32)),) * 2
        stats = lax.fori_loop(0, qi // 2, lambda j, s: blocks((2 * j, 2 * j + 1), s, False), init)
        stats = lax.cond(qi % 2 == 1,
                         lambda s: blocks((qi - 1, qi), s, True),
                         lambda s: blocks((qi,), s, True), stats)
        l0, l1 = stats[0][1], stats[1][1]
        ot = acc_ref[0] / l0 - lam * (acc_ref[1] / l1)
        ot = ot * lax.rsqrt(jnp.mean(ot * ot, axis=0, keepdims=True) + EPS)
        o_ref[out_slot] = (ot.T * gs_ref[...] * (1.0 - lam_init)).astype(BF16)

    i = pl.program_id(2)
    query_block(i, (q0a_ref, q1a_ref), 0)
    query_block(2 * pl.num_programs(2) - 1 - i, (q0b_ref, q1b_ref), 1)


def _diff_attn(q, k, vt, b_lambda, subln_g, j, lam_init, batch, seq):
    heads = q.shape[1] // 2
    _, nk, _, vd, t = vt.shape
    nq = seq // t
    assert nq % 2 == 0

    def q_spec(c, mirror):
        return pl.BlockSpec((None, None, t, HEAD_DIM),
                            lambda b, h, i: (b, c * heads + h, nq - 1 - i if mirror else i, 0))

    return pl.pallas_call(
        functools.partial(_diff_attn_kernel, lam_init=lam_init),
        grid=(batch, heads, nq // 2),
        in_specs=[
            pl.BlockSpec((None, 4, HEAD_DIM), lambda b, h, i: (j, 0, 0)),
            pl.BlockSpec((None, 1, vd), lambda b, h, i: (j, 0, 0)),
            q_spec(0, False), q_spec(1, False), q_spec(0, True), q_spec(1, True),
            pl.BlockSpec((None, None, seq, HEAD_DIM), lambda b, h, i: (b, h, 0, 0)),
            pl.BlockSpec((None, None, seq, HEAD_DIM), lambda b, h, i: (b, heads + h, 0, 0)),
            pl.BlockSpec((None, nk, None, vd, t), lambda b, h, i: (b, 0, h, 0, 0)),
        ],
        out_specs=pl.BlockSpec((None, 2, None, t, vd), lambda b, h, i: (b, 0, i, 0, h)),
        out_shape=jax.ShapeDtypeStruct((batch, 2, nq // 2, t, heads * vd), BF16),
        scratch_shapes=[pltpu.VMEM((2, vd, t), F32)],
        compiler_params=_params("parallel", "parallel", "parallel"),
        name="diff_attn",
    )(b_lambda, subln_g, q, q, q, q, k, k, vt)


def _out_proj_kernel(x_ref, mix_ref, mo_ref, w_ref, o_ref):
    mix_w = mix_ref.shape[1]
    tn = PROJ_TN
    for c in range(o_ref.shape[1] // tn):
        cols = slice(c * tn, (c + 1) * tn)
        y = (jnp.dot(mix_ref[...], w_ref[:mix_w, cols], preferred_element_type=F32)
             + jnp.dot(mo_ref[...], w_ref[mix_w:, cols], preferred_element_type=F32))
        o_ref[:, cols] = x_ref[:, cols] + y


def _out_proj(x, mix, mo, w_out, l):
    n, d = x.shape
    mix_w = mix.shape[-1]
    tm = OUT_TM
    if mix.ndim == 2:
        mix_spec = pl.BlockSpec((tm, mix_w), lambda r: (r, 0))
    else:
        _, _, half, t, _ = mix.shape
        assert t == tm

        def folded_block(r):
            b, qb = r // (2 * half), r % (2 * half)
            return (b, qb // half, jnp.where(qb < half, qb, 2 * half - 1 - qb), 0, 0)

        mix_spec = pl.BlockSpec((None, None, None, tm, mix_w), folded_block)
    return pl.pallas_call(
        _out_proj_kernel,
        grid=(n // tm,),
        in_specs=[
            pl.BlockSpec((tm, d), lambda r: (r, 0)),
            mix_spec,
            pl.BlockSpec((tm, MEM_W), lambda r: (r, 0)),
            pl.BlockSpec((None, d, d), lambda r: (l, 0, 0)),
        ],
        out_specs=pl.BlockSpec((tm, d), lambda r: (r, 0)),
        out_shape=jax.ShapeDtypeStruct((n, d), F32),
        compiler_params=_params("parallel"),
        name="out_proj",
    )(x, mix, mo, w_out)


def kernel(x, mem, norm_g, ffn_w_gu, ffn_w_down, w_out, mem_norm_g, mem_w_kv, mem_q_norm_g,
           mem_k_norm_g, a_w_in, a_v_norm_g, a_w_s, a_b_s, kv_norm_g, w_kv, k_norm_g, b_w_in,
           b_q_norm_g, b_lambda, b_subln_g):
    batch, seq, d = x.shape
    depth = norm_g.shape[0]
    n_a = a_w_in.shape[0]
    mix_w = d - MEM_W
    n = batch * seq

    w_out, mem_w_kv, a_w_in, w_kv, b_w_in = (
        w.astype(BF16) for w in (w_out, mem_w_kv, a_w_in, w_kv, b_w_in))
    ffn_w = (ffn_w_gu[0, 0].astype(BF16), ffn_w_down[0, 0].astype(BF16))

    def ffn(x, l, i, ffn_w):
        last = (l == depth - 1 and i == 1)
        nxt = None if last else (ffn_w_gu, ffn_w_down) + ((l, 1) if i == 0 else (l + 1, 0))
        res = _ffn(x, norm_g, l, 2 * i, ffn_w[0], ffn_w[1], nxt)
        return (res[0], None) if last else (res[0], (res[1], res[2]))

    norm_g = norm_g[:, :, None, :]
    mem_norm_g = mem_norm_g[:, None, :]
    mem_q_norm_g = mem_q_norm_g[:, None, :]
    mem_k_norm_g = mem_k_norm_g[:, None, :]
    a_v_norm_g = a_v_norm_g[:, None, :]
    b_q_norm_g = b_q_norm_g[:, None, :]
    b_subln_g = b_subln_g[:, None, :]
    a_b_s_t = jnp.swapaxes(a_b_s, 1, 2)

    pos = jnp.arange(seq, dtype=F32)
    inv = ROPE_THETA ** (-jnp.arange(0, HEAD_DIM, 2, dtype=F32) / HEAD_DIM)
    ang = pos[:, None] * inv[None, :]
    ang = jnp.concatenate([ang, ang], axis=-1)
    cos = jnp.cos(ang)
    sign = jnp.where(jnp.arange(HEAD_DIM) < HEAD_DIM // 2, -1.0, 1.0).astype(F32)
    sin_signed = jnp.sin(ang) * sign[None, :]

    x = x.reshape(n, d)
    mk, mv = _mem_kv(mem.reshape(batch * mem.shape[1], d), mem_norm_g, mem_w_kv, mem_k_norm_g)

    k_sh = v_sh = None
    for l in range(depth):
        if l == n_a:
            k_sh, v_sh = _kv_proj(x, kv_norm_g[None, :], w_kv, k_norm_g[None, :], cos, sin_signed,
                                  batch, seq, mix_w, b_subln_g.shape[-1])
        x, ffn_w = ffn(x, l, 0, ffn_w)
        if l < n_a:
            mix, mo = _mixer_a(x, norm_g, l, a_w_in, a_v_norm_g, a_w_s, a_b_s_t, mem_q_norm_g,
                               mk, mv, seq)
        else:
            j = l - n_a
            q, mo = _proj_b(x, norm_g, l, b_w_in, j, b_q_norm_g, cos, sin_signed, mem_q_norm_g,
                            mk, mv, seq)
            lam_init = 0.8 - 0.6 * math.exp(-0.3 * l)
            mix = _diff_attn(q, k_sh, v_sh, b_lambda, b_subln_g, j, lam_init, batch, seq)
        x = _out_proj(x, mix, mo, w_out, l)
        x, ffn_w = ffn(x, l, 1, ffn_w)
    return x.reshape(batch, seq, d)
```

```python
import functools
import math

import jax
import jax.numpy as jnp
from jax import lax
from jax.experimental import pallas as pl
from jax.experimental.pallas import tpu as pltpu

EPS = 1e-6
HEAD_DIM = 128
MEM_HEADS = 4
MEM_W = MEM_HEADS * HEAD_DIM
CHUNK = 128
GMLP_GROUPS = 6
ROPE_THETA = 10000.0
ATTN_SCALE = HEAD_DIM ** -0.5
SQRT_HALF = 0.7071067811865476

V7X_VMEM_BYTES = 64 * 1024 * 1024
VMEM_LIMIT_BYTES = V7X_VMEM_BYTES - 8 * 1024 * 1024
FFN_VMEM_LIMIT_BYTES = V7X_VMEM_BYTES - 5 * 1024 * 1024
FFN_TM = 1024
FFN_TF = 512
PROJ_TM = 512
PROJ_B_TM = 256
PROJ_SUB = 256
PROJ_TN = 512
OUT_TM = 512
ATTN_T = 512
ATTN_UNROLL = 4
LOG2E = 1.4426950408889634
Q_PRESCALE = ATTN_SCALE * LOG2E

F32 = jnp.float32
BF16 = jnp.bfloat16
NT_DIMS = (((1,), (1,)), ((), ()))


def _params(*semantics):
    return pltpu.CompilerParams(dimension_semantics=semantics,
                                vmem_limit_bytes=VMEM_LIMIT_BYTES)


def _rms(x, g):
    return x * lax.rsqrt(jnp.mean(x * x, axis=-1, keepdims=True) + EPS) * g


def _gelu(z):
    return 0.5 * z * (1.0 + lax.erf(z * SQRT_HALF))


def _rope(r, cos, sin_signed):
    return r * cos + pltpu.roll(r, HEAD_DIM // 2, 1) * sin_signed


def _mem_attn(zq, gq, mk_ref, mv_ref, mo_ref, rows):
    for hh in range(MEM_HEADS):
        cols = slice(hh * HEAD_DIM, (hh + 1) * HEAD_DIM)
        qh = _rms(zq[:, cols], gq).astype(BF16)
        s = lax.dot_general(qh, mk_ref[:, cols], NT_DIMS, preferred_element_type=F32) * ATTN_SCALE
        e = jnp.exp(s - jnp.max(s, axis=-1, keepdims=True))
        p = e / jnp.sum(e, axis=-1, keepdims=True)
        mo_ref[rows, cols] = jnp.dot(p.astype(BF16), mv_ref[:, cols],
                                     preferred_element_type=F32).astype(BF16)


def _sub_blocks(tm):
    return [slice(s, s + PROJ_SUB) for s in range(0, tm, PROJ_SUB)]


def _ffn_kernel(x_ref, g_ref, wg_ref, wu_ref, wd_ref, *refs):
    n_cast = (len(refs) - 2) // 2
    cast_in, o_ref, cast_out, h_ref = refs[:n_cast], refs[n_cast], refs[n_cast + 1:-1], refs[-1]
    j = pl.program_id(1)

    def step(first):
        if first:
            h_ref[...] = _rms(x_ref[...], g_ref[...]).astype(BF16)
        h = h_ref[...]
        g = jnp.dot(h, wg_ref[...], preferred_element_type=F32)
        u = jnp.dot(h, wu_ref[...], preferred_element_type=F32)
        for src_ref, dst_ref in zip(cast_in, cast_out):
            dst_ref[...] = src_ref[...].astype(BF16)
        act = (0.5 * g * jax.nn.sigmoid(g)) * u
        down = jnp.dot(act.astype(BF16), wd_ref[...], preferred_element_type=F32)
        if first:
            o_ref[...] = x_ref[...] + down
        else:
            o_ref[...] += down

    pl.when(j == 0)(functools.partial(step, True))
    pl.when(j > 0)(functools.partial(step, False))


def _ffn(x, norm_g, l, slot, w_gu, w_down, next_w=None):
    n, d = x.shape
    f = w_down.shape[0]
    tm, tf = FFN_TM, FFN_TF
    nr, nf = n // tm, f // tf
    in_specs = [
        pl.BlockSpec((tm, d), lambda r, j: (r, 0)),
        pl.BlockSpec((None, None, 1, d), lambda r, j: (l, slot, 0, 0)),
        pl.BlockSpec((d, tf), lambda r, j: (0, j)),
        pl.BlockSpec((d, tf), lambda r, j: (0, j + nf)),
        pl.BlockSpec((tf, d), lambda r, j: (j, 0)),
    ]
    out_specs = [pl.BlockSpec((tm, d), lambda r, j: (r, 0))]
    out_shape = [jax.ShapeDtypeStruct((n, d), F32)]
    args = [x, norm_g, w_gu, w_gu, w_down]
    if next_w is not None:
        gu32, down32, l2, i2 = next_w
        steps = nr * nf
        assert d % nr == 0 and (2 * f) % nf == 0 and f % steps == 0
        in_specs += [
            pl.BlockSpec((None, None, d // nr, 2 * f // nf), lambda r, j: (l2, i2, r, j)),
            pl.BlockSpec((None, None, f // steps, d), lambda r, j: (l2, i2, r * nf + j, 0)),
        ]
        out_specs += [pl.BlockSpec((d // nr, 2 * f // nf), lambda r, j: (r, j)),
                      pl.BlockSpec((f // steps, d), lambda r, j: (r * nf + j, 0))]
        out_shape += [jax.ShapeDtypeStruct((d, 2 * f), BF16), jax.ShapeDtypeStruct((f, d), BF16)]
        args += [gu32, down32]
    return pl.pallas_call(
        _ffn_kernel,
        grid=(nr, nf),
        in_specs=in_specs,
        out_specs=out_specs,
        out_shape=out_shape,
        scratch_shapes=[pltpu.VMEM((tm, d), BF16)],
        compiler_params=pltpu.CompilerParams(dimension_semantics=("parallel", "arbitrary"),
                                             vmem_limit_bytes=FFN_VMEM_LIMIT_BYTES),
        name="ffn",
    )(*args)


def _mem_kv_kernel(mem_ref, g_ref, w_ref, gk_ref, mk_ref, mv_ref):
    h = _rms(mem_ref[...], g_ref[...]).astype(BF16)
    kv = jnp.dot(h, w_ref[...], preferred_element_type=F32)
    gk = gk_ref[...]
    for hh in range(MEM_HEADS):
        cols = slice(hh * HEAD_DIM, (hh + 1) * HEAD_DIM)
        mk_ref[:, cols] = _rms(kv[:, cols], gk).astype(BF16)
    mv_ref[...] = kv[:, MEM_W:].astype(BF16)


def _mem_kv(mem2, mem_norm_g, mem_w_kv, mem_k_norm_g):
    rows, d = mem2.shape
    depth = mem_w_kv.shape[0]
    out = jax.ShapeDtypeStruct((depth, rows, MEM_W), BF16)
    return pl.pallas_call(
        _mem_kv_kernel,
        grid=(depth,),
        in_specs=[
            pl.BlockSpec((rows, d), lambda l: (0, 0)),
            pl.BlockSpec((None, 1, d), lambda l: (l, 0, 0)),
            pl.BlockSpec((None, d, 2 * MEM_W), lambda l: (l, 0, 0)),
            pl.BlockSpec((None, 1, HEAD_DIM), lambda l: (l, 0, 0)),
        ],
        out_specs=[pl.BlockSpec((None, rows, MEM_W), lambda l: (l, 0, 0))] * 2,
        out_shape=[out, out],
        compiler_params=_params("parallel"),
        name="mem_kv",
    )(mem2, mem_norm_g, mem_w_kv, mem_k_norm_g)


def _mixer_a_kernel(x_ref, g_ref, w_ref, gv_ref, ws_ref, bs_ref, gq_ref, mk_ref, mv_ref,
                    mix_ref, mo_ref, h_ref, u_ref, v_ref):
    tm = x_ref.shape[0]
    mix_w = u_ref.shape[1]
    tn = PROJ_TN
    gw = mix_w // GMLP_GROUPS
    causal = (lax.broadcasted_iota(jnp.int32, (CHUNK, CHUNK), 0)
              >= lax.broadcasted_iota(jnp.int32, (CHUNK, CHUNK), 1))
    w_s = [jnp.where(causal, ws_ref[grp], 0.0).astype(BF16) for grp in range(GMLP_GROUPS)]
    h_ref[...] = _rms(x_ref[...], g_ref[...]).astype(BF16)
    for c in range(mix_w // tn):
        cols = slice(c * tn, (c + 1) * tn)
        u_ref[:, cols] = _gelu(jnp.dot(h_ref[...], w_ref[:, cols], preferred_element_type=F32))
        v_ref[:, cols] = _gelu(jnp.dot(h_ref[...], w_ref[:, mix_w + c * tn:mix_w + (c + 1) * tn],
                                       preferred_element_type=F32))
    zq = jnp.dot(h_ref[...], w_ref[:, 2 * mix_w:], preferred_element_type=F32)
    _mem_attn(zq, gq_ref[...], mk_ref, mv_ref, mo_ref, slice(0, tm))
    for t in range(0, tm, CHUNK):
        chunk = slice(t, t + CHUNK)
        vn = _rms(v_ref[chunk, :], gv_ref[...]).astype(BF16)
        for grp in range(GMLP_GROUPS):
            cols = slice(grp * gw, (grp + 1) * gw)
            mixed = (jnp.dot(w_s[grp], vn[:, cols], preferred_element_type=F32)
                     + bs_ref[:, grp:grp + 1])
            mix_ref[chunk, cols] = (u_ref[chunk, cols] * mixed).astype(BF16)


def _mixer_a(x, norm_g, l, w_in, v_norm_g, w_s, b_s_t, mem_q_norm_g, mk, mv, seq):
    n, d = x.shape
    mix_w = v_norm_g.shape[-1]
    tm = PROJ_TM
    blocks_per_seq = seq // tm
    mem_len = mk.shape[1] // (n // seq)
    mem_spec = pl.BlockSpec((None, mem_len, MEM_W), lambda r: (l, r // blocks_per_seq, 0))
    return pl.pallas_call(
        _mixer_a_kernel,
        grid=(n // tm,),
        in_specs=[
            pl.BlockSpec((tm, d), lambda r: (r, 0)),
            pl.BlockSpec((None, None, 1, d), lambda r: (l, 1, 0, 0)),
            pl.BlockSpec((None, d, 2 * mix_w + MEM_W), lambda r: (l, 0, 0)),
            pl.BlockSpec((None, 1, mix_w), lambda r: (l, 0, 0)),
            pl.BlockSpec((None, GMLP_GROUPS, CHUNK, CHUNK), lambda r: (l, 0, 0, 0)),
            pl.BlockSpec((None, CHUNK, GMLP_GROUPS), lambda r: (l, 0, 0)),
            pl.BlockSpec((None, 1, HEAD_DIM), lambda r: (l, 0, 0)),
            mem_spec, mem_spec,
        ],
        out_specs=[pl.BlockSpec((tm, mix_w), lambda r: (r, 0)),
                   pl.BlockSpec((tm, MEM_W), lambda r: (r, 0))],
        out_shape=[jax.ShapeDtypeStruct((n, mix_w), BF16),
                   jax.ShapeDtypeStruct((n, MEM_W), BF16)],
        scratch_shapes=[pltpu.VMEM((tm, d), BF16),
                        pltpu.VMEM((tm, mix_w), F32),
                        pltpu.VMEM((tm, mix_w), F32)],
        compiler_params=_params("parallel"),
        name="mixer_a",
    )(x, norm_g, w_in, v_norm_g, w_s, b_s_t, mem_q_norm_g, mk, mv)


def _kv_proj_kernel(x_ref, g_ref, w_ref, gk_ref, cos_ref, sin_ref, k_ref, vt_ref, h_ref, v_ref):
    qk_w = k_ref.shape[0] * HEAD_DIM
    heads, vd, tm = vt_ref.shape
    tn = PROJ_TN
    gk = gk_ref[...]
    for rows in _sub_blocks(tm):
        h_ref[rows, :] = _rms(x_ref[rows, :], g_ref[...]).astype(BF16)
    for rows in _sub_blocks(tm):
        cos, sin = cos_ref[rows, :], sin_ref[rows, :]
        for c in range(qk_w // tn):
            z = jnp.dot(h_ref[rows, :], w_ref[:, c * tn:(c + 1) * tn], preferred_element_type=F32)
            for k in range(tn // HEAD_DIM):
                r = _rms(z[:, k * HEAD_DIM:(k + 1) * HEAD_DIM], gk)
                k_ref[c * (tn // HEAD_DIM) + k, rows, :] = _rope(r, cos, sin).astype(BF16)
        for hh in range(heads):
            v_ref[rows, :] = jnp.dot(h_ref[rows, :], w_ref[:, qk_w + hh * vd:qk_w + (hh + 1) * vd],
                                     preferred_element_type=F32)
            vt_ref[hh, :, rows] = v_ref[rows, :].T.astype(BF16)


def _kv_proj(x, kv_norm_g, w_kv, k_norm_g, cos, sin_signed, batch, seq, qk_w, vd):
    n, d = x.shape
    heads = (w_kv.shape[1] - qk_w) // vd
    tm = ATTN_T
    blocks_per_seq = seq // tm
    rope_spec = pl.BlockSpec((tm, HEAD_DIM), lambda r: (r % blocks_per_seq, 0))
    return pl.pallas_call(
        _kv_proj_kernel,
        grid=(n // tm,),
        in_specs=[
            pl.BlockSpec((tm, d), lambda r: (r, 0)),
            pl.BlockSpec((1, d), lambda r: (0, 0)),
            pl.BlockSpec((d, qk_w + heads * vd), lambda r: (0, 0)),
            pl.BlockSpec((1, HEAD_DIM), lambda r: (0, 0)),
            rope_spec, rope_spec,
        ],
        out_specs=[pl.BlockSpec((None, qk_w // HEAD_DIM, tm, HEAD_DIM),
                                lambda r: (r // blocks_per_seq, 0, r % blocks_per_seq, 0)),
                   pl.BlockSpec((None, None, heads, vd, tm),
                                lambda r: (r // blocks_per_seq, r % blocks_per_seq, 0, 0, 0))],
        out_shape=[jax.ShapeDtypeStruct((batch, qk_w // HEAD_DIM, seq, HEAD_DIM), BF16),
                   jax.ShapeDtypeStruct((batch, blocks_per_seq, heads, vd, tm), BF16)],
        scratch_shapes=[pltpu.VMEM((tm, d), BF16), pltpu.VMEM((tm, vd), F32)],
        compiler_params=_params("parallel"),
        name="kv_proj",
    )(x, kv_norm_g, w_kv, k_norm_g, cos, sin_signed)


def _proj_b_kernel(x_ref, g_ref, w_ref, gqn_ref, cos_ref, sin_ref, gq_ref, mk_ref, mv_ref,
                   q_ref, mo_ref, h_ref):
    n_qk, tm, _ = q_ref.shape
    qk_w = n_qk * HEAD_DIM
    tn = PROJ_TN
    gqn = gqn_ref[...]
    for rows in _sub_blocks(tm):
        h_ref[rows, :] = _rms(x_ref[rows, :], g_ref[...]).astype(BF16)
    for rows in _sub_blocks(tm):
        cos, sin = cos_ref[rows, :], sin_ref[rows, :]
        for c in range(qk_w // tn):
            z = jnp.dot(h_ref[rows, :], w_ref[:, c * tn:(c + 1) * tn], preferred_element_type=F32)
            for k in range(tn // HEAD_DIM):
                r = _rms(z[:, k * HEAD_DIM:(k + 1) * HEAD_DIM], gqn)
                q_ref[c * (tn // HEAD_DIM) + k, rows, :] = (
                    _rope(r, cos, sin) * Q_PRESCALE).astype(BF16)
        zq = jnp.dot(h_ref[rows, :], w_ref[:, qk_w:], preferred_element_type=F32)
        _mem_attn(zq, gq_ref[...], mk_ref, mv_ref, mo_ref, rows)


def _proj_b(x, norm_g, l, w_in, j, q_norm_g, cos, sin_signed, mem_q_norm_g, mk, mv, seq):
    n, d = x.shape
    qk_w = w_in.shape[2] - MEM_W
    tm = PROJ_B_TM
    blocks_per_seq = seq // tm
    mem_len = mk.shape[1] // (n // seq)
    rope_spec = pl.BlockSpec((tm, HEAD_DIM), lambda r: (r % blocks_per_seq, 0))
    mem_spec = pl.BlockSpec((None, mem_len, MEM_W), lambda r: (l, r // blocks_per_seq, 0))
    return pl.pallas_call(
        _proj_b_kernel,
        grid=(n // tm,),
        in_specs=[
            pl.BlockSpec((tm, d), lambda r: (r, 0)),
            pl.BlockSpec((None, None, 1, d), lambda r: (l, 1, 0, 0)),
            pl.BlockSpec((None, d, qk_w + MEM_W), lambda r: (j, 0, 0)),
            pl.BlockSpec((None, 1, HEAD_DIM), lambda r: (j, 0, 0)),
            rope_spec, rope_spec,
            pl.BlockSpec((None, 1, HEAD_DIM), lambda r: (l, 0, 0)),
            mem_spec, mem_spec,
        ],
        out_specs=[pl.BlockSpec((None, qk_w // HEAD_DIM, tm, HEAD_DIM),
                                lambda r: (r // blocks_per_seq, 0, r % blocks_per_seq, 0)),
                   pl.BlockSpec((tm, MEM_W), lambda r: (r, 0))],
        out_shape=[jax.ShapeDtypeStruct((n // seq, qk_w // HEAD_DIM, seq, HEAD_DIM), BF16),
                   jax.ShapeDtypeStruct((n, MEM_W), BF16)],
        scratch_shapes=[pltpu.VMEM((tm, d), BF16)],
        compiler_params=_params("parallel"),
        name="proj_b",
    )(x, norm_g, w_in, q_norm_g, cos, sin_signed, mem_q_norm_g, mk, mv)


def _diff_attn_kernel(lam_ref, gs_ref, q0a_ref, q1a_ref, q0b_ref, q1b_ref, k0_ref, k1_ref, vt_ref,
                      o_ref, acc_ref, *, lam_init):
    t = ATTN_T
    k_refs = (k0_ref, k1_ref)
    lp = lam_ref[...]
    lam = (jnp.exp(jnp.sum(lp[0:1] * lp[1:2], axis=-1, keepdims=True))
           - jnp.exp(jnp.sum(lp[2:3] * lp[3:4], axis=-1, keepdims=True)) + lam_init)

    def query_block(qi, q_refs, out_slot):
        acc_ref[...] = jnp.zeros(acc_ref.shape, F32)

        def scores(ki, c):
            rows = pl.ds(pl.multiple_of(ki * t, t), t)
            return lax.dot_general(k_refs[c][rows, :], q_refs[c][...], NT_DIMS,
                                   preferred_element_type=F32)

        def update(ki, c, st, stat, masked):
            m_old, l_old = stat
            if masked:
                key_le_query = (lax.broadcasted_iota(jnp.int32, (t, t), 0)
                                <= lax.broadcasted_iota(jnp.int32, (t, t), 1))
                st = jnp.where(key_le_query, st, -jnp.inf)
            m_new = jnp.maximum(m_old, jnp.max(st, axis=0, keepdims=True))
            alpha = jnp.exp2(m_old - m_new)
            pt = jnp.exp2(st - m_new)
            acc_ref[c] = alpha * acc_ref[c] + jnp.dot(vt_ref[ki], pt.astype(BF16),
                                                      preferred_element_type=F32)
            return m_new, alpha * l_old + jnp.sum(pt, axis=0, keepdims=True)

        def blocks(kis, stats, masked_last):
            stats = list(stats)
            work = [(ki, c, masked_last and n == len(kis) - 1)
                    for n, ki in enumerate(kis) for c in range(2)]
            pending = [scores(ki, c) for ki, c, _ in work[:2]]
            for n, (ki, c, masked) in enumerate(work):
                if n + 2 < len(work):
                    pending.append(scores(work[n + 2][0], work[n + 2][1]))
                stats[c] = update(ki, c, pending[n], stats[c], masked)
            return tuple(stats)

        init = ((jnp.full((1, t), -jnp.inf, F32), jnp.zeros((1, t), F32)),) * 2
        u = ATTN_UNROLL
        stats = lax.fori_loop(
            0, qi // u, lambda j, s: blocks(tuple(u * j + n for n in range(u)), s, False), init)
        base = (qi // u) * u
        stats = lax.switch(
            qi % u,
            [functools.partial(lambda r, s: blocks(tuple(base + n for n in range(r)) + (qi,), s, True), r)
             for r in range(u)],
            stats)
        l0, l1 = stats[0][1], stats[1][1]
        ot = acc_ref[0] / l0 - lam * (acc_ref[1] / l1)
        ot = ot * lax.rsqrt(jnp.mean(ot * ot, axis=0, keepdims=True) + EPS)
        o_ref[out_slot] = (ot.T * gs_ref[...] * (1.0 - lam_init)).astype(BF16)

    i = pl.program_id(2)
    query_block(i, (q0a_ref, q1a_ref), 0)
    query_block(2 * pl.num_programs(2) - 1 - i, (q0b_ref, q1b_ref), 1)


def _diff_attn(q, k, vt, b_lambda, subln_g, j, lam_init, batch, seq):
    heads = q.shape[1] // 2
    _, nk, _, vd, t = vt.shape
    nq = seq // t
    assert nq % 2 == 0

    def q_spec(c, mirror):
        return pl.BlockSpec((None, None, t, HEAD_DIM),
                            lambda b, h, i: (b, c * heads + h, nq - 1 - i if mirror else i, 0))

    return pl.pallas_call(
        functools.partial(_diff_attn_kernel, lam_init=lam_init),
        grid=(batch, heads, nq // 2),
        in_specs=[
            pl.BlockSpec((None, 4, HEAD_DIM), lambda b, h, i: (j, 0, 0)),
            pl.BlockSpec((None, 1, vd), lambda b, h, i: (j, 0, 0)),
            q_spec(0, False), q_spec(1, False), q_spec(0, True), q_spec(1, True),
            pl.BlockSpec((None, None, seq, HEAD_DIM), lambda b, h, i: (b, h, 0, 0)),
            pl.BlockSpec((None, None, seq, HEAD_DIM), lambda b, h, i: (b, heads + h, 0, 0)),
            pl.BlockSpec((None, nk, None, vd, t), lambda b, h, i: (b, 0, h, 0, 0)),
        ],
        out_specs=pl.BlockSpec((None, 2, None, t, vd), lambda b, h, i: (b, 0, i, 0, h)),
        out_shape=jax.ShapeDtypeStruct((batch, 2, nq // 2, t, heads * vd), BF16),
        scratch_shapes=[pltpu.VMEM((2, vd, t), F32)],
        compiler_params=_params("parallel", "parallel", "parallel"),
        name="diff_attn",
    )(b_lambda, subln_g, q, q, q, q, k, k, vt)


def _out_proj_kernel(x_ref, mix_ref, mo_ref, w_ref, o_ref):
    mix_w = mix_ref.shape[1]
    tn = PROJ_TN
    for c in range(o_ref.shape[1] // tn):
        cols = slice(c * tn, (c + 1) * tn)
        y = (jnp.dot(mix_ref[...], w_ref[:mix_w, cols], preferred_element_type=F32)
             + jnp.dot(mo_ref[...], w_ref[mix_w:, cols], preferred_element_type=F32))
        o_ref[:, cols] = x_ref[:, cols] + y


def _out_proj(x, mix, mo, w_out, l):
    n, d = x.shape
    mix_w = mix.shape[-1]
    tm = OUT_TM
    if mix.ndim == 2:
        mix_spec = pl.BlockSpec((tm, mix_w), lambda r: (r, 0))
    else:
        _, _, half, t, _ = mix.shape
        assert t == tm

        def folded_block(r):
            b, qb = r // (2 * half), r % (2 * half)
            return (b, qb // half, jnp.where(qb < half, qb, 2 * half - 1 - qb), 0, 0)

        mix_spec = pl.BlockSpec((None, None, None, tm, mix_w), folded_block)
    return pl.pallas_call(
        _out_proj_kernel,
        grid=(n // tm,),
        in_specs=[
            pl.BlockSpec((tm, d), lambda r: (r, 0)),
            mix_spec,
            pl.BlockSpec((tm, MEM_W), lambda r: (r, 0)),
            pl.BlockSpec((None, d, d), lambda r: (l, 0, 0)),
        ],
        out_specs=pl.BlockSpec((tm, d), lambda r: (r, 0)),
        out_shape=jax.ShapeDtypeStruct((n, d), F32),
        compiler_params=_params("parallel"),
        name="out_proj",
    )(x, mix, mo, w_out)


def kernel(x, mem, norm_g, ffn_w_gu, ffn_w_down, w_out, mem_norm_g, mem_w_kv, mem_q_norm_g,
           mem_k_norm_g, a_w_in, a_v_norm_g, a_w_s, a_b_s, kv_norm_g, w_kv, k_norm_g, b_w_in,
           b_q_norm_g, b_lambda, b_subln_g):
    batch, seq, d = x.shape
    depth = norm_g.shape[0]
    n_a = a_w_in.shape[0]
    mix_w = d - MEM_W
    n = batch * seq

    w_out, mem_w_kv, a_w_in, w_kv, b_w_in = (
        w.astype(BF16) for w in (w_out, mem_w_kv, a_w_in, w_kv, b_w_in))
    ffn_w = (ffn_w_gu[0, 0].astype(BF16), ffn_w_down[0, 0].astype(BF16))

    def ffn(x, l, i, ffn_w):
        last = (l == depth - 1 and i == 1)
        nxt = None if last else (ffn_w_gu, ffn_w_down) + ((l, 1) if i == 0 else (l + 1, 0))
        res = _ffn(x, norm_g, l, 2 * i, ffn_w[0], ffn_w[1], nxt)
        return (res[0], None) if last else (res[0], (res[1], res[2]))

    norm_g = norm_g[:, :, None, :]
    mem_norm_g = mem_norm_g[:, None, :]
    mem_q_norm_g = mem_q_norm_g[:, None, :]
    mem_k_norm_g = mem_k_norm_g[:, None, :]
    a_v_norm_g = a_v_norm_g[:, None, :]
    b_q_norm_g = b_q_norm_g[:, None, :]
    b_subln_g = b_subln_g[:, None, :]
    a_b_s_t = jnp.swapaxes(a_b_s, 1, 2)

    pos = jnp.arange(seq, dtype=F32)
    inv = ROPE_THETA ** (-jnp.arange(0, HEAD_DIM, 2, dtype=F32) / HEAD_DIM)
    ang = pos[:, None] * inv[None, :]
    ang = jnp.concatenate([ang, ang], axis=-1)
    cos = jnp.cos(ang)
    sign = jnp.where(jnp.arange(HEAD_DIM) < HEAD_DIM // 2, -1.0, 1.0).astype(F32)
    sin_signed = jnp.sin(ang) * sign[None, :]

    x = x.reshape(n, d)
    mk, mv = _mem_kv(mem.reshape(batch * mem.shape[1], d), mem_norm_g, mem_w_kv, mem_k_norm_g)

    k_sh = v_sh = None
    for l in range(depth):
        if l == n_a:
            k_sh, v_sh = _kv_proj(x, kv_norm_g[None, :], w_kv, k_norm_g[None, :], cos, sin_signed,
                                  batch, seq, mix_w, b_subln_g.shape[-1])
        x, ffn_w = ffn(x, l, 0, ffn_w)
        if l < n_a:
            mix, mo = _mixer_a(x, norm_g, l, a_w_in, a_v_norm_g, a_w_s, a_b_s_t, mem_q_norm_g,
                               mk, mv, seq)
        else:
            j = l - n_a
            q, mo = _proj_b(x, norm_g, l, b_w_in, j, b_q_norm_g, cos, sin_signed, mem_q_norm_g,
                            mk, mv, seq)
            lam_init = 0.8 - 0.6 * math.exp(-0.3 * l)
            mix = _diff_attn(q, k_sh, v_sh, b_lambda, b_subln_g, j, lam_init, batch, seq)
        x = _out_proj(x, mix, mo, w_out, l)
        x, ffn_w = ffn(x, l, 1, ffn_w)
    return x.reshape(batch, seq, d)
```

```python
import functools
import math

import jax
import jax.numpy as jnp
from jax import lax
from jax.experimental import pallas as pl
from jax.experimental.pallas import tpu as pltpu

EPS = 1e-6
HEAD_DIM = 128
MEM_HEADS = 4
MEM_W = MEM_HEADS * HEAD_DIM
CHUNK = 128
GMLP_GROUPS = 6
ROPE_THETA = 10000.0
ATTN_SCALE = HEAD_DIM ** -0.5
SQRT_HALF = 0.7071067811865476

V7X_VMEM_BYTES = 64 * 1024 * 1024
VMEM_LIMIT_BYTES = V7X_VMEM_BYTES - 8 * 1024 * 1024
FFN_VMEM_LIMIT_BYTES = V7X_VMEM_BYTES - 5 * 1024 * 1024
FFN_TM = 1024
FFN_TF = 512
PROJ_TM = 512
PROJ_B_TM = 256
PROJ_SUB = 256
PROJ_TN = 512
OUT_TM = 512
ATTN_T = 512
ATTN_UNROLL = 8
LOG2E = 1.4426950408889634
Q_PRESCALE = ATTN_SCALE * LOG2E

F32 = jnp.float32
BF16 = jnp.bfloat16
NT_DIMS = (((1,), (1,)), ((), ()))


def _params(*semantics):
    return pltpu.CompilerParams(dimension_semantics=semantics,
                                vmem_limit_bytes=VMEM_LIMIT_BYTES)


def _rms(x, g):
    return x * lax.rsqrt(jnp.mean(x * x, axis=-1, keepdims=True) + EPS) * g


def _gelu(z):
    return 0.5 * z * (1.0 + lax.erf(z * SQRT_HALF))


def _rope(r, cos, sin_signed):
    return r * cos + pltpu.roll(r, HEAD_DIM // 2, 1) * sin_signed


def _mem_attn(zq, gq, mk_ref, mv_ref, mo_ref, rows):
    for hh in range(MEM_HEADS):
        cols = slice(hh * HEAD_DIM, (hh + 1) * HEAD_DIM)
        qh = _rms(zq[:, cols], gq).astype(BF16)
        s = lax.dot_general(qh, mk_ref[:, cols], NT_DIMS, preferred_element_type=F32) * ATTN_SCALE
        e = jnp.exp(s - jnp.max(s, axis=-1, keepdims=True))
        p = e / jnp.sum(e, axis=-1, keepdims=True)
        mo_ref[rows, cols] = jnp.dot(p.astype(BF16), mv_ref[:, cols],
                                     preferred_element_type=F32).astype(BF16)


def _sub_blocks(tm):
    return [slice(s, s + PROJ_SUB) for s in range(0, tm, PROJ_SUB)]


def _ffn_kernel(x_ref, g_ref, wg_ref, wu_ref, wd_ref, *refs):
    n_cast = (len(refs) - 2) // 2
    cast_in, o_ref, cast_out, h_ref = refs[:n_cast], refs[n_cast], refs[n_cast + 1:-1], refs[-1]
    j = pl.program_id(1)

    def step(first):
        if first:
            h_ref[...] = _rms(x_ref[...], g_ref[...]).astype(BF16)
        h = h_ref[...]
        g = jnp.dot(h, wg_ref[...], preferred_element_type=F32)
        u = jnp.dot(h, wu_ref[...], preferred_element_type=F32)
        for src_ref, dst_ref in zip(cast_in, cast_out):
            dst_ref[...] = src_ref[...].astype(BF16)
        act = (0.5 * g * jax.nn.sigmoid(g)) * u
        down = jnp.dot(act.astype(BF16), wd_ref[...], preferred_element_type=F32)
        if first:
            o_ref[...] = x_ref[...] + down
        else:
            o_ref[...] += down

    pl.when(j == 0)(functools.partial(step, True))
    pl.when(j > 0)(functools.partial(step, False))


def _ffn(x, norm_g, l, slot, w_gu, w_down, next_w=None):
    n, d = x.shape
    f = w_down.shape[0]
    tm, tf = FFN_TM, FFN_TF
    nr, nf = n // tm, f // tf
    in_specs = [
        pl.BlockSpec((tm, d), lambda r, j: (r, 0)),
        pl.BlockSpec((None, None, 1, d), lambda r, j: (l, slot, 0, 0)),
        pl.BlockSpec((d, tf), lambda r, j: (0, j)),
        pl.BlockSpec((d, tf), lambda r, j: (0, j + nf)),
        pl.BlockSpec((tf, d), lambda r, j: (j, 0)),
    ]
    out_specs = [pl.BlockSpec((tm, d), lambda r, j: (r, 0))]
    out_shape = [jax.ShapeDtypeStruct((n, d), F32)]
    args = [x, norm_g, w_gu, w_gu, w_down]
    if next_w is not None:
        gu32, down32, l2, i2 = next_w
        steps = nr * nf
        assert d % nr == 0 and (2 * f) % nf == 0 and f % steps == 0
        in_specs += [
            pl.BlockSpec((None, None, d // nr, 2 * f // nf), lambda r, j: (l2, i2, r, j)),
            pl.BlockSpec((None, None, f // steps, d), lambda r, j: (l2, i2, r * nf + j, 0)),
        ]
        out_specs += [pl.BlockSpec((d // nr, 2 * f // nf), lambda r, j: (r, j)),
                      pl.BlockSpec((f // steps, d), lambda r, j: (r * nf + j, 0))]
        out_shape += [jax.ShapeDtypeStruct((d, 2 * f), BF16), jax.ShapeDtypeStruct((f, d), BF16)]
        args += [gu32, down32]
    return pl.pallas_call(
        _ffn_kernel,
        grid=(nr, nf),
        in_specs=in_specs,
        out_specs=out_specs,
        out_shape=out_shape,
        scratch_shapes=[pltpu.VMEM((tm, d), BF16)],
        compiler_params=pltpu.CompilerParams(dimension_semantics=("parallel", "arbitrary"),
                                             vmem_limit_bytes=FFN_VMEM_LIMIT_BYTES),
        name="ffn",
    )(*args)


def _mem_kv_kernel(mem_ref, g_ref, w_ref, gk_ref, mk_ref, mv_ref):
    h = _rms(mem_ref[...], g_ref[...]).astype(BF16)
    kv = jnp.dot(h, w_ref[...], preferred_element_type=F32)
    gk = gk_ref[...]
    for hh in range(MEM_HEADS):
        cols = slice(hh * HEAD_DIM, (hh + 1) * HEAD_DIM)
        mk_ref[:, cols] = _rms(kv[:, cols], gk).astype(BF16)
    mv_ref[...] = kv[:, MEM_W:].astype(BF16)


def _mem_kv(mem2, mem_norm_g, mem_w_kv, mem_k_norm_g):
    rows, d = mem2.shape
    depth = mem_w_kv.shape[0]
    out = jax.ShapeDtypeStruct((depth, rows, MEM_W), BF16)
    return pl.pallas_call(
        _mem_kv_kernel,
        grid=(depth,),
        in_specs=[
            pl.BlockSpec((rows, d), lambda l: (0, 0)),
            pl.BlockSpec((None, 1, d), lambda l: (l, 0, 0)),
            pl.BlockSpec((None, d, 2 * MEM_W), lambda l: (l, 0, 0)),
            pl.BlockSpec((None, 1, HEAD_DIM), lambda l: (l, 0, 0)),
        ],
        out_specs=[pl.BlockSpec((None, rows, MEM_W), lambda l: (l, 0, 0))] * 2,
        out_shape=[out, out],
        compiler_params=_params("parallel"),
        name="mem_kv",
    )(mem2, mem_norm_g, mem_w_kv, mem_k_norm_g)


def _mixer_a_kernel(x_ref, g_ref, w_ref, gv_ref, ws_ref, bs_ref, gq_ref, mk_ref, mv_ref,
                    mix_ref, mo_ref, h_ref, u_ref, v_ref):
    tm = x_ref.shape[0]
    mix_w = u_ref.shape[1]
    tn = PROJ_TN
    gw = mix_w // GMLP_GROUPS
    causal = (lax.broadcasted_iota(jnp.int32, (CHUNK, CHUNK), 0)
              >= lax.broadcasted_iota(jnp.int32, (CHUNK, CHUNK), 1))
    w_s = [jnp.where(causal, ws_ref[grp], 0.0).astype(BF16) for grp in range(GMLP_GROUPS)]
    h_ref[...] = _rms(x_ref[...], g_ref[...]).astype(BF16)
    for c in range(mix_w // tn):
        cols = slice(c * tn, (c + 1) * tn)
        u_ref[:, cols] = _gelu(jnp.dot(h_ref[...], w_ref[:, cols], preferred_element_type=F32))
        v_ref[:, cols] = _gelu(jnp.dot(h_ref[...], w_ref[:, mix_w + c * tn:mix_w + (c + 1) * tn],
                                       preferred_element_type=F32))
    zq = jnp.dot(h_ref[...], w_ref[:, 2 * mix_w:], preferred_element_type=F32)
    _mem_attn(zq, gq_ref[...], mk_ref, mv_ref, mo_ref, slice(0, tm))
    for t in range(0, tm, CHUNK):
        chunk = slice(t, t + CHUNK)
        vn = _rms(v_ref[chunk, :], gv_ref[...]).astype(BF16)
        for grp in range(GMLP_GROUPS):
            cols = slice(grp * gw, (grp + 1) * gw)
            mixed = (jnp.dot(w_s[grp], vn[:, cols], preferred_element_type=F32)
                     + bs_ref[:, grp:grp + 1])
            mix_ref[chunk, cols] = (u_ref[chunk, cols] * mixed).astype(BF16)


def _mixer_a(x, norm_g, l, w_in, v_norm_g, w_s, b_s_t, mem_q_norm_g, mk, mv, seq):
    n, d = x.shape
    mix_w = v_norm_g.shape[-1]
    tm = PROJ_TM
    blocks_per_seq = seq // tm
    mem_len = mk.shape[1] // (n // seq)
    mem_spec = pl.BlockSpec((None, mem_len, MEM_W), lambda r: (l, r // blocks_per_seq, 0))
    return pl.pallas_call(
        _mixer_a_kernel,
        grid=(n // tm,),
        in_specs=[
            pl.BlockSpec((tm, d), lambda r: (r, 0)),
            pl.BlockSpec((None, None, 1, d), lambda r: (l, 1, 0, 0)),
            pl.BlockSpec((None, d, 2 * mix_w + MEM_W), lambda r: (l, 0, 0)),
            pl.BlockSpec((None, 1, mix_w), lambda r: (l, 0, 0)),
            pl.BlockSpec((None, GMLP_GROUPS, CHUNK, CHUNK), lambda r: (l, 0, 0, 0)),
            pl.BlockSpec((None, CHUNK, GMLP_GROUPS), lambda r: (l, 0, 0)),
            pl.BlockSpec((None, 1, HEAD_DIM), lambda r: (l, 0, 0)),
            mem_spec, mem_spec,
        ],
        out_specs=[pl.BlockSpec((tm, mix_w), lambda r: (r, 0)),
                   pl.BlockSpec((tm, MEM_W), lambda r: (r, 0))],
        out_shape=[jax.ShapeDtypeStruct((n, mix_w), BF16),
                   jax.ShapeDtypeStruct((n, MEM_W), BF16)],
        scratch_shapes=[pltpu.VMEM((tm, d), BF16),
                        pltpu.VMEM((tm, mix_w), F32),
                        pltpu.VMEM((tm, mix_w), F32)],
        compiler_params=_params("parallel"),
        name="mixer_a",
    )(x, norm_g, w_in, v_norm_g, w_s, b_s_t, mem_q_norm_g, mk, mv)


def _kv_proj_kernel(x_ref, g_ref, w_ref, gk_ref, cos_ref, sin_ref, k_ref, vt_ref, h_ref, v_ref):
    qk_w = k_ref.shape[0] * HEAD_DIM
    heads, vd, tm = vt_ref.shape
    tn = PROJ_TN
    gk = gk_ref[...]
    for rows in _sub_blocks(tm):
        h_ref[rows, :] = _rms(x_ref[rows, :], g_ref[...]).astype(BF16)
    for rows in _sub_blocks(tm):
        cos, sin = cos_ref[rows, :], sin_ref[rows, :]
        for c in range(qk_w // tn):
            z = jnp.dot(h_ref[rows, :], w_ref[:, c * tn:(c + 1) * tn], preferred_element_type=F32)
            for k in range(tn // HEAD_DIM):
                r = _rms(z[:, k * HEAD_DIM:(k + 1) * HEAD_DIM], gk)
                k_ref[c * (tn // HEAD_DIM) + k, rows, :] = _rope(r, cos, sin).astype(BF16)
        for hh in range(heads):
            v_ref[rows, :] = jnp.dot(h_ref[rows, :], w_ref[:, qk_w + hh * vd:qk_w + (hh + 1) * vd],
                                     preferred_element_type=F32)
            vt_ref[hh, :, rows] = v_ref[rows, :].T.astype(BF16)


def _kv_proj(x, kv_norm_g, w_kv, k_norm_g, cos, sin_signed, batch, seq, qk_w, vd):
    n, d = x.shape
    heads = (w_kv.shape[1] - qk_w) // vd
    tm = ATTN_T
    blocks_per_seq = seq // tm
    rope_spec = pl.BlockSpec((tm, HEAD_DIM), lambda r: (r % blocks_per_seq, 0))
    return pl.pallas_call(
        _kv_proj_kernel,
        grid=(n // tm,),
        in_specs=[
            pl.BlockSpec((tm, d), lambda r: (r, 0)),
            pl.BlockSpec((1, d), lambda r: (0, 0)),
            pl.BlockSpec((d, qk_w + heads * vd), lambda r: (0, 0)),
            pl.BlockSpec((1, HEAD_DIM), lambda r: (0, 0)),
            rope_spec, rope_spec,
        ],
        out_specs=[pl.BlockSpec((None, qk_w // HEAD_DIM, tm, HEAD_DIM),
                                lambda r: (r // blocks_per_seq, 0, r % blocks_per_seq, 0)),
                   pl.BlockSpec((None, None, heads, vd, tm),
                                lambda r: (r // blocks_per_seq, r % blocks_per_seq, 0, 0, 0))],
        out_shape=[jax.ShapeDtypeStruct((batch, qk_w // HEAD_DIM, seq, HEAD_DIM), BF16),
                   jax.ShapeDtypeStruct((batch, blocks_per_seq, heads, vd, tm), BF16)],
        scratch_shapes=[pltpu.VMEM((tm, d), BF16), pltpu.VMEM((tm, vd), F32)],
        compiler_params=_params("parallel"),
        name="kv_proj",
    )(x, kv_norm_g, w_kv, k_norm_g, cos, sin_signed)


def _proj_b_kernel(x_ref, g_ref, w_ref, gqn_ref, cos_ref, sin_ref, gq_ref, mk_ref, mv_ref,
                   q_ref, mo_ref, h_ref):
    n_qk, tm, _ = q_ref.shape
    qk_w = n_qk * HEAD_DIM
    tn = PROJ_TN
    gqn = gqn_ref[...]
    for rows in _sub_blocks(tm):
        h_ref[rows, :] = _rms(x_ref[rows, :], g_ref[...]).astype(BF16)
    for rows in _sub_blocks(tm):
        cos, sin = cos_ref[rows, :], sin_ref[rows, :]
        for c in range(qk_w // tn):
            z = jnp.dot(h_ref[rows, :], w_ref[:, c * tn:(c + 1) * tn], preferred_element_type=F32)
            for k in range(tn // HEAD_DIM):
                r = _rms(z[:, k * HEAD_DIM:(k + 1) * HEAD_DIM], gqn)
                q_ref[c * (tn // HEAD_DIM) + k, rows, :] = (
                    _rope(r, cos, sin) * Q_PRESCALE).astype(BF16)
        zq = jnp.dot(h_ref[rows, :], w_ref[:, qk_w:], preferred_element_type=F32)
        _mem_attn(zq, gq_ref[...], mk_ref, mv_ref, mo_ref, rows)


def _proj_b(x, norm_g, l, w_in, j, q_norm_g, cos, sin_signed, mem_q_norm_g, mk, mv, seq):
    n, d = x.shape
    qk_w = w_in.shape[2] - MEM_W
    tm = PROJ_B_TM
    blocks_per_seq = seq // tm
    mem_len = mk.shape[1] // (n // seq)
    rope_spec = pl.BlockSpec((tm, HEAD_DIM), lambda r: (r % blocks_per_seq, 0))
    mem_spec = pl.BlockSpec((None, mem_len, MEM_W), lambda r: (l, r // blocks_per_seq, 0))
    return pl.pallas_call(
        _proj_b_kernel,
        grid=(n // tm,),
        in_specs=[
            pl.BlockSpec((tm, d), lambda r: (r, 0)),
            pl.BlockSpec((None, None, 1, d), lambda r: (l, 1, 0, 0)),
            pl.BlockSpec((None, d, qk_w + MEM_W), lambda r: (j, 0, 0)),
            pl.BlockSpec((None, 1, HEAD_DIM), lambda r: (j, 0, 0)),
            rope_spec, rope_spec,
            pl.BlockSpec((None, 1, HEAD_DIM), lambda r: (l, 0, 0)),
            mem_spec, mem_spec,
        ],
        out_specs=[pl.BlockSpec((None, qk_w // HEAD_DIM, tm, HEAD_DIM),
                                lambda r: (r // blocks_per_seq, 0, r % blocks_per_seq, 0)),
                   pl.BlockSpec((tm, MEM_W), lambda r: (r, 0))],
        out_shape=[jax.ShapeDtypeStruct((n // seq, qk_w // HEAD_DIM, seq, HEAD_DIM), BF16),
                   jax.ShapeDtypeStruct((n, MEM_W), BF16)],
        scratch_shapes=[pltpu.VMEM((tm, d), BF16)],
        compiler_params=_params("parallel"),
        name="proj_b",
    )(x, norm_g, w_in, q_norm_g, cos, sin_signed, mem_q_norm_g, mk, mv)


def _diff_attn_kernel(lam_ref, gs_ref, q0a_ref, q1a_ref, q0b_ref, q1b_ref, k0_ref, k1_ref, vt_ref,
                      o_ref, acc_ref, *, lam_init, nq):
    t = ATTN_T
    k_refs = (k0_ref, k1_ref)
    lp = lam_ref[...]
    lam = (jnp.exp(jnp.sum(lp[0:1] * lp[1:2], axis=-1, keepdims=True))
           - jnp.exp(jnp.sum(lp[2:3] * lp[3:4], axis=-1, keepdims=True)) + lam_init)

    def query_block(qi, q_refs, out_slot):
        acc_ref[...] = jnp.zeros(acc_ref.shape, F32)

        def scores(ki, c):
            rows = pl.ds(ki * t if isinstance(ki, int) else pl.multiple_of(ki * t, t), t)
            return lax.dot_general(k_refs[c][rows, :], q_refs[c][...], NT_DIMS,
                                   preferred_element_type=F32)

        def update(ki, c, st, stat, masked):
            m_old, l_old = stat
            if masked:
                key_le_query = (lax.broadcasted_iota(jnp.int32, (t, t), 0)
                                <= lax.broadcasted_iota(jnp.int32, (t, t), 1))
                st = jnp.where(key_le_query, st, -jnp.inf)
            m_new = jnp.maximum(m_old, jnp.max(st, axis=0, keepdims=True))
            alpha = jnp.exp2(m_old - m_new)
            pt = jnp.exp2(st - m_new)
            acc_ref[c] = alpha * acc_ref[c] + jnp.dot(vt_ref[ki], pt.astype(BF16),
                                                      preferred_element_type=F32)
            return m_new, alpha * l_old + jnp.sum(pt, axis=0, keepdims=True)

        def blocks(kis, stats, masked_last):
            stats = list(stats)
            work = [(ki, c, masked_last and n == len(kis) - 1)
                    for n, ki in enumerate(kis) for c in range(2)]
            pending = [scores(ki, c) for ki, c, _ in work[:2]]
            for n, (ki, c, masked) in enumerate(work):
                if n + 2 < len(work):
                    pending.append(scores(work[n + 2][0], work[n + 2][1]))
                stats[c] = update(ki, c, pending[n], stats[c], masked)
            return tuple(stats)

        init = ((jnp.full((1, t), -jnp.inf, F32), jnp.zeros((1, t), F32)),) * 2
        u = min(ATTN_UNROLL, nq)
        if u < nq:
            stats = lax.fori_loop(
                0, qi // u, lambda j, s: blocks(tuple(u * j + n for n in range(u)), s, False), init)
            base, diag = (qi // u) * u, qi
        else:
            stats, base, diag = init, 0, None
        stats = lax.switch(
            qi % u,
            [functools.partial(
                lambda r, s: blocks(tuple(base + n for n in range(r)) + (r if diag is None else diag,),
                                    s, True), r)
             for r in range(u)],
            stats)
        l0, l1 = stats[0][1], stats[1][1]
        ot = acc_ref[0] / l0 - lam * (acc_ref[1] / l1)
        ot = ot * lax.rsqrt(jnp.mean(ot * ot, axis=0, keepdims=True) + EPS)
        o_ref[out_slot] = (ot.T * gs_ref[...] * (1.0 - lam_init)).astype(BF16)

    i = pl.program_id(2)
    query_block(i, (q0a_ref, q1a_ref), 0)
    query_block(nq - 1 - i, (q0b_ref, q1b_ref), 1)


def _diff_attn(q, k, vt, b_lambda, subln_g, j, lam_init, batch, seq):
    heads = q.shape[1] // 2
    _, nk, _, vd, t = vt.shape
    nq = seq // t
    assert nq % 2 == 0

    def q_spec(c, mirror):
        return pl.BlockSpec((None, None, t, HEAD_DIM),
                            lambda b, h, i: (b, c * heads + h, nq - 1 - i if mirror else i, 0))

    return pl.pallas_call(
        functools.partial(_diff_attn_kernel, lam_init=lam_init, nq=nq),
        grid=(batch, heads, nq // 2),
        in_specs=[
            pl.BlockSpec((None, 4, HEAD_DIM), lambda b, h, i: (j, 0, 0)),
            pl.BlockSpec((None, 1, vd), lambda b, h, i: (j, 0, 0)),
            q_spec(0, False), q_spec(1, False), q_spec(0, True), q_spec(1, True),
            pl.BlockSpec((None, None, seq, HEAD_DIM), lambda b, h, i: (b, h, 0, 0)),
            pl.BlockSpec((None, None, seq, HEAD_DIM), lambda b, h, i: (b, heads + h, 0, 0)),
            pl.BlockSpec((None, nk, None, vd, t), lambda b, h, i: (b, 0, h, 0, 0)),
        ],
        out_specs=pl.BlockSpec((None, 2, None, t, vd), lambda b, h, i: (b, 0, i, 0, h)),
        out_shape=jax.ShapeDtypeStruct((batch, 2, nq // 2, t, heads * vd), BF16),
        scratch_shapes=[pltpu.VMEM((2, vd, t), F32)],
        compiler_params=_params("parallel", "parallel", "parallel"),
        name="diff_attn",
    )(b_lambda, subln_g, q, q, q, q, k, k, vt)


def _out_proj_kernel(x_ref, mix_ref, mo_ref, w_ref, o_ref):
    mix_w = mix_ref.shape[1]
    tn = PROJ_TN
    for c in range(o_ref.shape[1] // tn):
        cols = slice(c * tn, (c + 1) * tn)
        y = (jnp.dot(mix_ref[...], w_ref[:mix_w, cols], preferred_element_type=F32)
             + jnp.dot(mo_ref[...], w_ref[mix_w:, cols], preferred_element_type=F32))
        o_ref[:, cols] = x_ref[:, cols] + y


def _out_proj(x, mix, mo, w_out, l):
    n, d = x.shape
    mix_w = mix.shape[-1]
    tm = OUT_TM
    if mix.ndim == 2:
        mix_spec = pl.BlockSpec((tm, mix_w), lambda r: (r, 0))
    else:
        _, _, half, t, _ = mix.shape
        assert t == tm

        def folded_block(r):
            b, qb = r // (2 * half), r % (2 * half)
            return (b, qb // half, jnp.where(qb < half, qb, 2 * half - 1 - qb), 0, 0)

        mix_spec = pl.BlockSpec((None, None, None, tm, mix_w), folded_block)
    return pl.pallas_call(
        _out_proj_kernel,
        grid=(n // tm,),
        in_specs=[
            pl.BlockSpec((tm, d), lambda r: (r, 0)),
            mix_spec,
            pl.BlockSpec((tm, MEM_W), lambda r: (r, 0)),
            pl.BlockSpec((None, d, d), lambda r: (l, 0, 0)),
        ],
        out_specs=pl.BlockSpec((tm, d), lambda r: (r, 0)),
        out_shape=jax.ShapeDtypeStruct((n, d), F32),
        compiler_params=_params("parallel"),
        name="out_proj",
    )(x, mix, mo, w_out)


def kernel(x, mem, norm_g, ffn_w_gu, ffn_w_down, w_out, mem_norm_g, mem_w_kv, mem_q_norm_g,
           mem_k_norm_g, a_w_in, a_v_norm_g, a_w_s, a_b_s, kv_norm_g, w_kv, k_norm_g, b_w_in,
           b_q_norm_g, b_lambda, b_subln_g):
    batch, seq, d = x.shape
    depth = norm_g.shape[0]
    n_a = a_w_in.shape[0]
    mix_w = d - MEM_W
    n = batch * seq

    w_out, mem_w_kv, a_w_in, w_kv, b_w_in = (
        w.astype(BF16) for w in (w_out, mem_w_kv, a_w_in, w_kv, b_w_in))
    ffn_w = (ffn_w_gu[0, 0].astype(BF16), ffn_w_down[0, 0].astype(BF16))

    def ffn(x, l, i, ffn_w):
        last = (l == depth - 1 and i == 1)
        nxt = None if last else (ffn_w_gu, ffn_w_down) + ((l, 1) if i == 0 else (l + 1, 0))
        res = _ffn(x, norm_g, l, 2 * i, ffn_w[0], ffn_w[1], nxt)
        return (res[0], None) if last else (res[0], (res[1], res[2]))

    norm_g = norm_g[:, :, None, :]
    mem_norm_g = mem_norm_g[:, None, :]
    mem_q_norm_g = mem_q_norm_g[:, None, :]
    mem_k_norm_g = mem_k_norm_g[:, None, :]
    a_v_norm_g = a_v_norm_g[:, None, :]
    b_q_norm_g = b_q_norm_g[:, None, :]
    b_subln_g = b_subln_g[:, None, :]
    a_b_s_t = jnp.swapaxes(a_b_s, 1, 2)

    pos = jnp.arange(seq, dtype=F32)
    inv = ROPE_THETA ** (-jnp.arange(0, HEAD_DIM, 2, dtype=F32) / HEAD_DIM)
    ang = pos[:, None] * inv[None, :]
    ang = jnp.concatenate([ang, ang], axis=-1)
    cos = jnp.cos(ang)
    sign = jnp.where(jnp.arange(HEAD_DIM) < HEAD_DIM // 2, -1.0, 1.0).astype(F32)
    sin_signed = jnp.sin(ang) * sign[None, :]

    x = x.reshape(n, d)
    mk, mv = _mem_kv(mem.reshape(batch * mem.shape[1], d), mem_norm_g, mem_w_kv, mem_k_norm_g)

    k_sh = v_sh = None
    for l in range(depth):
        if l == n_a:
            k_sh, v_sh = _kv_proj(x, kv_norm_g[None, :], w_kv, k_norm_g[None, :], cos, sin_signed,
                                  batch, seq, mix_w, b_subln_g.shape[-1])
        x, ffn_w = ffn(x, l, 0, ffn_w)
        if l < n_a:
            mix, mo = _mixer_a(x, norm_g, l, a_w_in, a_v_norm_g, a_w_s, a_b_s_t, mem_q_norm_g,
                               mk, mv, seq)
        else:
            j = l - n_a
            q, mo = _proj_b(x, norm_g, l, b_w_in, j, b_q_norm_g, cos, sin_signed, mem_q_norm_g,
                            mk, mv, seq)
            lam_init = 0.8 - 0.6 * math.exp(-0.3 * l)
            mix = _diff_attn(q, k_sh, v_sh, b_lambda, b_subln_g, j, lam_init, batch, seq)
        x = _out_proj(x, mix, mo, w_out, l)
        x, ffn_w = ffn(x, l, 1, ffn_w)
    return x.reshape(batch, seq, d)
```

```python
import functools
import math

import jax
import jax.numpy as jnp
from jax import lax
from jax.experimental import pallas as pl
from jax.experimental.pallas import tpu as pltpu

EPS = 1e-6
HEAD_DIM = 128
MEM_HEADS = 4
MEM_W = MEM_HEADS * HEAD_DIM
CHUNK = 128
GMLP_GROUPS = 6
ROPE_THETA = 10000.0
ATTN_SCALE = HEAD_DIM ** -0.5
SQRT_HALF = 0.7071067811865476

V7X_VMEM_BYTES = 64 * 1024 * 1024
VMEM_LIMIT_BYTES = V7X_VMEM_BYTES - 8 * 1024 * 1024
FFN_VMEM_LIMIT_BYTES = V7X_VMEM_BYTES - 4 * 1024 * 1024
FFN_TM = 1024
FFN_TF = 512
FFN_CAST_SHARE = 2
PROJ_TM = 512
PROJ_B_TM = 256
PROJ_SUB = 256
PROJ_TN = 512
OUT_TM = 512
ATTN_T = 512
ATTN_UNROLL = 4
LOG2E = 1.4426950408889634
Q_PRESCALE = ATTN_SCALE * LOG2E

F32 = jnp.float32
BF16 = jnp.bfloat16
NT_DIMS = (((1,), (1,)), ((), ()))


def _params(*semantics):
    return pltpu.CompilerParams(dimension_semantics=semantics,
                                vmem_limit_bytes=VMEM_LIMIT_BYTES)


def _rms(x, g):
    return x * lax.rsqrt(jnp.mean(x * x, axis=-1, keepdims=True) + EPS) * g


def _gelu(z):
    return 0.5 * z * (1.0 + lax.erf(z * SQRT_HALF))


def _rope(r, cos, sin_signed):
    return r * cos + pltpu.roll(r, HEAD_DIM // 2, 1) * sin_signed


def _mem_attn(zq, gq, mk_ref, mv_ref, mo_ref, rows):
    for hh in range(MEM_HEADS):
        cols = slice(hh * HEAD_DIM, (hh + 1) * HEAD_DIM)
        qh = _rms(zq[:, cols], gq).astype(BF16)
        s = lax.dot_general(qh, mk_ref[:, cols], NT_DIMS, preferred_element_type=F32) * ATTN_SCALE
        e = jnp.exp(s - jnp.max(s, axis=-1, keepdims=True))
        p = e / jnp.sum(e, axis=-1, keepdims=True)
        mo_ref[rows, cols] = jnp.dot(p.astype(BF16), mv_ref[:, cols],
                                     preferred_element_type=F32).astype(BF16)


def _sub_blocks(tm):
    return [slice(s, s + PROJ_SUB) for s in range(0, tm, PROJ_SUB)]


def _ffn_kernel(x_ref, g_ref, wg_ref, wu_ref, wd_ref, *refs):
    n_cast = (len(refs) - 2) // 2
    cast_in, o_ref, cast_out, h_ref = refs[:n_cast], refs[n_cast], refs[n_cast + 1:-1], refs[-1]
    j = pl.program_id(1)

    def step(first):
        if first:
            h_ref[...] = _rms(x_ref[...], g_ref[...]).astype(BF16)
        h = h_ref[...]
        g = jnp.dot(h, wg_ref[...], preferred_element_type=F32)
        u = jnp.dot(h, wu_ref[...], preferred_element_type=F32)
        for src_ref, dst_ref in zip(cast_in, cast_out):
            dst_ref[...] = src_ref[...].astype(BF16)
        act = (0.5 * g * jax.nn.sigmoid(g)) * u
        down = jnp.dot(act.astype(BF16), wd_ref[...], preferred_element_type=F32)
        if first:
            o_ref[...] = x_ref[...] + down
        else:
            o_ref[...] += down

    pl.when(j == 0)(functools.partial(step, True))
    pl.when(j > 0)(functools.partial(step, False))


def _ffn(x, norm_g, l, slot, w_gu, w_down, next_w=None):
    n, d = x.shape
    f = w_down.shape[0]
    tm, tf = FFN_TM, FFN_TF
    nr, nf = n // tm, f // tf
    in_specs = [
        pl.BlockSpec((tm, d), lambda r, j: (r, 0)),
        pl.BlockSpec((None, None, 1, d), lambda r, j: (l, slot, 0, 0)),
        pl.BlockSpec((d, tf), lambda r, j: (0, j)),
        pl.BlockSpec((d, tf), lambda r, j: (0, j + nf)),
        pl.BlockSpec((tf, d), lambda r, j: (j, 0)),
    ]
    out_specs = [pl.BlockSpec((tm, d), lambda r, j: (r, 0))]
    out_shape = [jax.ShapeDtypeStruct((n, d), F32)]
    args = [x, norm_g, w_gu, w_gu, w_down]
    if next_w is not None:
        gu32, down32, l2, i2 = next_w
        pairs = nr * nf // FFN_CAST_SHARE
        slabs = pairs // nf
        assert nr * nf % FFN_CAST_SHARE == 0 and pairs % nf == 0
        assert d % slabs == 0 and (2 * f) % nf == 0 and f % pairs == 0

        def pair(r, j):
            return (r * nf + j) // FFN_CAST_SHARE

        in_specs += [
            pl.BlockSpec((None, None, d // slabs, 2 * f // nf),
                         lambda r, j: (l2, i2, pair(r, j) // nf, pair(r, j) % nf)),
            pl.BlockSpec((None, None, f // pairs, d), lambda r, j: (l2, i2, pair(r, j), 0)),
        ]
        out_specs += [pl.BlockSpec((d // slabs, 2 * f // nf),
                                   lambda r, j: (pair(r, j) // nf, pair(r, j) % nf)),
                      pl.BlockSpec((f // pairs, d), lambda r, j: (pair(r, j), 0))]
        out_shape += [jax.ShapeDtypeStruct((d, 2 * f), BF16), jax.ShapeDtypeStruct((f, d), BF16)]
        args += [gu32, down32]
    return pl.pallas_call(
        _ffn_kernel,
        grid=(nr, nf),
        in_specs=in_specs,
        out_specs=out_specs,
        out_shape=out_shape,
        scratch_shapes=[pltpu.VMEM((tm, d), BF16)],
        compiler_params=pltpu.CompilerParams(dimension_semantics=("parallel", "arbitrary"),
                                             vmem_limit_bytes=FFN_VMEM_LIMIT_BYTES),
        name="ffn",
    )(*args)


def _mem_kv_kernel(mem_ref, g_ref, w_ref, gk_ref, mk_ref, mv_ref):
    h = _rms(mem_ref[...], g_ref[...]).astype(BF16)
    kv = jnp.dot(h, w_ref[...], preferred_element_type=F32)
    gk = gk_ref[...]
    for hh in range(MEM_HEADS):
        cols = slice(hh * HEAD_DIM, (hh + 1) * HEAD_DIM)
        mk_ref[:, cols] = _rms(kv[:, cols], gk).astype(BF16)
    mv_ref[...] = kv[:, MEM_W:].astype(BF16)


def _mem_kv(mem2, mem_norm_g, mem_w_kv, mem_k_norm_g):
    rows, d = mem2.shape
    depth = mem_w_kv.shape[0]
    out = jax.ShapeDtypeStruct((depth, rows, MEM_W), BF16)
    return pl.pallas_call(
        _mem_kv_kernel,
        grid=(depth,),
        in_specs=[
            pl.BlockSpec((rows, d), lambda l: (0, 0)),
            pl.BlockSpec((None, 1, d), lambda l: (l, 0, 0)),
            pl.BlockSpec((None, d, 2 * MEM_W), lambda l: (l, 0, 0)),
            pl.BlockSpec((None, 1, HEAD_DIM), lambda l: (l, 0, 0)),
        ],
        out_specs=[pl.BlockSpec((None, rows, MEM_W), lambda l: (l, 0, 0))] * 2,
        out_shape=[out, out],
        compiler_params=_params("parallel"),
        name="mem_kv",
    )(mem2, mem_norm_g, mem_w_kv, mem_k_norm_g)


def _mixer_a_kernel(x_ref, g_ref, w_ref, gv_ref, ws_ref, bs_ref, gq_ref, mk_ref, mv_ref,
                    mix_ref, mo_ref, h_ref, u_ref, v_ref):
    tm = x_ref.shape[0]
    mix_w = u_ref.shape[1]
    tn = PROJ_TN
    gw = mix_w // GMLP_GROUPS
    causal = (lax.broadcasted_iota(jnp.int32, (CHUNK, CHUNK), 0)
              >= lax.broadcasted_iota(jnp.int32, (CHUNK, CHUNK), 1))
    w_s = [jnp.where(causal, ws_ref[grp], 0.0).astype(BF16) for grp in range(GMLP_GROUPS)]
    h_ref[...] = _rms(x_ref[...], g_ref[...]).astype(BF16)
    for c in range(mix_w // tn):
        cols = slice(c * tn, (c + 1) * tn)
        u_ref[:, cols] = _gelu(jnp.dot(h_ref[...], w_ref[:, cols], preferred_element_type=F32))
        v_ref[:, cols] = _gelu(jnp.dot(h_ref[...], w_ref[:, mix_w + c * tn:mix_w + (c + 1) * tn],
                                       preferred_element_type=F32))
    zq = jnp.dot(h_ref[...], w_ref[:, 2 * mix_w:], preferred_element_type=F32)
    _mem_attn(zq, gq_ref[...], mk_ref, mv_ref, mo_ref, slice(0, tm))
    for t in range(0, tm, CHUNK):
        chunk = slice(t, t + CHUNK)
        vn = _rms(v_ref[chunk, :], gv_ref[...]).astype(BF16)
        for grp in range(GMLP_GROUPS):
            cols = slice(grp * gw, (grp + 1) * gw)
            mixed = (jnp.dot(w_s[grp], vn[:, cols], preferred_element_type=F32)
                     + bs_ref[:, grp:grp + 1])
            mix_ref[chunk, cols] = (u_ref[chunk, cols] * mixed).astype(BF16)


def _mixer_a(x, norm_g, l, w_in, v_norm_g, w_s, b_s_t, mem_q_norm_g, mk, mv, seq):
    n, d = x.shape
    mix_w = v_norm_g.shape[-1]
    tm = PROJ_TM
    blocks_per_seq = seq // tm
    mem_len = mk.shape[1] // (n // seq)
    mem_spec = pl.BlockSpec((None, mem_len, MEM_W), lambda r: (l, r // blocks_per_seq, 0))
    return pl.pallas_call(
        _mixer_a_kernel,
        grid=(n // tm,),
        in_specs=[
            pl.BlockSpec((tm, d), lambda r: (r, 0)),
            pl.BlockSpec((None, None, 1, d), lambda r: (l, 1, 0, 0)),
            pl.BlockSpec((None, d, 2 * mix_w + MEM_W), lambda r: (l, 0, 0)),
            pl.BlockSpec((None, 1, mix_w), lambda r: (l, 0, 0)),
            pl.BlockSpec((None, GMLP_GROUPS, CHUNK, CHUNK), lambda r: (l, 0, 0, 0)),
            pl.BlockSpec((None, CHUNK, GMLP_GROUPS), lambda r: (l, 0, 0)),
            pl.BlockSpec((None, 1, HEAD_DIM), lambda r: (l, 0, 0)),
            mem_spec, mem_spec,
        ],
        out_specs=[pl.BlockSpec((tm, mix_w), lambda r: (r, 0)),
                   pl.BlockSpec((tm, MEM_W), lambda r: (r, 0))],
        out_shape=[jax.ShapeDtypeStruct((n, mix_w), BF16),
                   jax.ShapeDtypeStruct((n, MEM_W), BF16)],
        scratch_shapes=[pltpu.VMEM((tm, d), BF16),
                        pltpu.VMEM((tm, mix_w), F32),
                        pltpu.VMEM((tm, mix_w), F32)],
        compiler_params=_params("parallel"),
        name="mixer_a",
    )(x, norm_g, w_in, v_norm_g, w_s, b_s_t, mem_q_norm_g, mk, mv)


def _kv_proj_kernel(x_ref, g_ref, w_ref, gk_ref, cos_ref, sin_ref, k_ref, vt_ref, h_ref, v_ref):
    qk_w = k_ref.shape[0] * HEAD_DIM
    heads, vd, tm = vt_ref.shape
    tn = PROJ_TN
    gk = gk_ref[...]
    for rows in _sub_blocks(tm):
        h_ref[rows, :] = _rms(x_ref[rows, :], g_ref[...]).astype(BF16)
    for rows in _sub_blocks(tm):
        cos, sin = cos_ref[rows, :], sin_ref[rows, :]
        for c in range(qk_w // tn):
            z = jnp.dot(h_ref[rows, :], w_ref[:, c * tn:(c + 1) * tn], preferred_element_type=F32)
            for k in range(tn // HEAD_DIM):
                r = _rms(z[:, k * HEAD_DIM:(k + 1) * HEAD_DIM], gk)
                k_ref[c * (tn // HEAD_DIM) + k, rows, :] = _rope(r, cos, sin).astype(BF16)
        for hh in range(heads):
            v_ref[rows, :] = jnp.dot(h_ref[rows, :], w_ref[:, qk_w + hh * vd:qk_w + (hh + 1) * vd],
                                     preferred_element_type=F32)
            vt_ref[hh, :, rows] = v_ref[rows, :].T.astype(BF16)


def _kv_proj(x, kv_norm_g, w_kv, k_norm_g, cos, sin_signed, batch, seq, qk_w, vd):
    n, d = x.shape
    heads = (w_kv.shape[1] - qk_w) // vd
    tm = ATTN_T
    blocks_per_seq = seq // tm
    rope_spec = pl.BlockSpec((tm, HEAD_DIM), lambda r: (r % blocks_per_seq, 0))
    return pl.pallas_call(
        _kv_proj_kernel,
        grid=(n // tm,),
        in_specs=[
            pl.BlockSpec((tm, d), lambda r: (r, 0)),
            pl.BlockSpec((1, d), lambda r: (0, 0)),
            pl.BlockSpec((d, qk_w + heads * vd), lambda r: (0, 0)),
            pl.BlockSpec((1, HEAD_DIM), lambda r: (0, 0)),
            rope_spec, rope_spec,
        ],
        out_specs=[pl.BlockSpec((None, qk_w // HEAD_DIM, tm, HEAD_DIM),
                                lambda r: (r // blocks_per_seq, 0, r % blocks_per_seq, 0)),
                   pl.BlockSpec((None, None, heads, vd, tm),
                                lambda r: (r // blocks_per_seq, r % blocks_per_seq, 0, 0, 0))],
        out_shape=[jax.ShapeDtypeStruct((batch, qk_w // HEAD_DIM, seq, HEAD_DIM), BF16),
                   jax.ShapeDtypeStruct((batch, blocks_per_seq, heads, vd, tm), BF16)],
        scratch_shapes=[pltpu.VMEM((tm, d), BF16), pltpu.VMEM((tm, vd), F32)],
        compiler_params=_params("parallel"),
        name="kv_proj",
    )(x, kv_norm_g, w_kv, k_norm_g, cos, sin_signed)


def _proj_b_kernel(x_ref, g_ref, w_ref, gqn_ref, cos_ref, sin_ref, gq_ref, mk_ref, mv_ref,
                   q_ref, mo_ref, h_ref):
    n_qk, tm, _ = q_ref.shape
    qk_w = n_qk * HEAD_DIM
    tn = PROJ_TN
    gqn = gqn_ref[...]
    for rows in _sub_blocks(tm):
        h_ref[rows, :] = _rms(x_ref[rows, :], g_ref[...]).astype(BF16)
    for rows in _sub_blocks(tm):
        cos, sin = cos_ref[rows, :], sin_ref[rows, :]
        for c in range(qk_w // tn):
            z = jnp.dot(h_ref[rows, :], w_ref[:, c * tn:(c + 1) * tn], preferred_element_type=F32)
            for k in range(tn // HEAD_DIM):
                r = _rms(z[:, k * HEAD_DIM:(k + 1) * HEAD_DIM], gqn)
                q_ref[c * (tn // HEAD_DIM) + k, rows, :] = (
                    _rope(r, cos, sin) * Q_PRESCALE).astype(BF16)
        zq = jnp.dot(h_ref[rows, :], w_ref[:, qk_w:], preferred_element_type=F32)
        _mem_attn(zq, gq_ref[...], mk_ref, mv_ref, mo_ref, rows)


def _proj_b(x, norm_g, l, w_in, j, q_norm_g, cos, sin_signed, mem_q_norm_g, mk, mv, seq):
    n, d = x.shape
    qk_w = w_in.shape[2] - MEM_W
    tm = PROJ_B_TM
    blocks_per_seq = seq // tm
    mem_len = mk.shape[1] // (n // seq)
    rope_spec = pl.BlockSpec((tm, HEAD_DIM), lambda r: (r % blocks_per_seq, 0))
    mem_spec = pl.BlockSpec((None, mem_len, MEM_W), lambda r: (l, r // blocks_per_seq, 0))
    return pl.pallas_call(
        _proj_b_kernel,
        grid=(n // tm,),
        in_specs=[
            pl.BlockSpec((tm, d), lambda r: (r, 0)),
            pl.BlockSpec((None, None, 1, d), lambda r: (l, 1, 0, 0)),
            pl.BlockSpec((None, d, qk_w + MEM_W), lambda r: (j, 0, 0)),
            pl.BlockSpec((None, 1, HEAD_DIM), lambda r: (j, 0, 0)),
            rope_spec, rope_spec,
            pl.BlockSpec((None, 1, HEAD_DIM), lambda r: (l, 0, 0)),
            mem_spec, mem_spec,
        ],
        out_specs=[pl.BlockSpec((None, qk_w // HEAD_DIM, tm, HEAD_DIM),
                                lambda r: (r // blocks_per_seq, 0, r % blocks_per_seq, 0)),
                   pl.BlockSpec((tm, MEM_W), lambda r: (r, 0))],
        out_shape=[jax.ShapeDtypeStruct((n // seq, qk_w // HEAD_DIM, seq, HEAD_DIM), BF16),
                   jax.ShapeDtypeStruct((n, MEM_W), BF16)],
        scratch_shapes=[pltpu.VMEM((tm, d), BF16)],
        compiler_params=_params("parallel"),
        name="proj_b",
    )(x, norm_g, w_in, q_norm_g, cos, sin_signed, mem_q_norm_g, mk, mv)


def _diff_attn_kernel(lam_ref, gs_ref, q0a_ref, q1a_ref, q0b_ref, q1b_ref, k0_ref, k1_ref, vt_ref,
                      o_ref, acc_ref, *, lam_init):
    t = ATTN_T
    k_refs = (k0_ref, k1_ref)
    lp = lam_ref[...]
    lam = (jnp.exp(jnp.sum(lp[0:1] * lp[1:2], axis=-1, keepdims=True))
           - jnp.exp(jnp.sum(lp[2:3] * lp[3:4], axis=-1, keepdims=True)) + lam_init)

    def query_block(qi, q_refs, out_slot):
        acc_ref[...] = jnp.zeros(acc_ref.shape, F32)

        def scores(ki, c):
            rows = pl.ds(pl.multiple_of(ki * t, t), t)
            return lax.dot_general(k_refs[c][rows, :], q_refs[c][...], NT_DIMS,
                                   preferred_element_type=F32)

        def update(ki, c, st, stat, masked):
            m_old, l_old = stat
            if masked:
                key_le_query = (lax.broadcasted_iota(jnp.int32, (t, t), 0)
                                <= lax.broadcasted_iota(jnp.int32, (t, t), 1))
                st = jnp.where(key_le_query, st, -jnp.inf)
            m_new = jnp.maximum(m_old, jnp.max(st, axis=0, keepdims=True))
            alpha = jnp.exp2(m_old - m_new)
            pt = jnp.exp2(st - m_new)
            acc_ref[c] = alpha * acc_ref[c] + jnp.dot(vt_ref[ki], pt.astype(BF16),
                                                      preferred_element_type=F32)
            return m_new, alpha * l_old + jnp.sum(pt, axis=0, keepdims=True)

        def blocks(kis, stats, masked_last):
            stats = list(stats)
            work = [(ki, c, masked_last and n == len(kis) - 1)
                    for n, ki in enumerate(kis) for c in range(2)]
            pending = [scores(ki, c) for ki, c, _ in work[:2]]
            for n, (ki, c, masked) in enumerate(work):
                if n + 2 < len(work):
                    pending.append(scores(work[n + 2][0], work[n + 2][1]))
                stats[c] = update(ki, c, pending[n], stats[c], masked)
            return tuple(stats)

        init = ((jnp.full((1, t), -jnp.inf, F32), jnp.zeros((1, t), F32)),) * 2
        u = ATTN_UNROLL
        stats = lax.fori_loop(
            0, qi // u, lambda j, s: blocks(tuple(u * j + n for n in range(u)), s, False), init)
        base = (qi // u) * u
        stats = lax.switch(
            qi % u,
            [functools.partial(lambda r, s: blocks(tuple(base + n for n in range(r)) + (qi,), s, True), r)
             for r in range(u)],
            stats)
        l0, l1 = stats[0][1], stats[1][1]
        ot = acc_ref[0] / l0 - lam * (acc_ref[1] / l1)
        ot = ot * lax.rsqrt(jnp.mean(ot * ot, axis=0, keepdims=True) + EPS)
        o_ref[out_slot] = (ot.T * gs_ref[...] * (1.0 - lam_init)).astype(BF16)

    i = pl.program_id(2)
    query_block(i, (q0a_ref, q1a_ref), 0)
    query_block(2 * pl.num_programs(2) - 1 - i, (q0b_ref, q1b_ref), 1)


def _diff_attn(q, k, vt, b_lambda, subln_g, j, lam_init, batch, seq):
    heads = q.shape[1] // 2
    _, nk, _, vd, t = vt.shape
    nq = seq // t
    assert nq % 2 == 0

    def q_spec(c, mirror):
        return pl.BlockSpec((None, None, t, HEAD_DIM),
                            lambda b, h, i: (b, c * heads + h, nq - 1 - i if mirror else i, 0))

    return pl.pallas_call(
        functools.partial(_diff_attn_kernel, lam_init=lam_init),
        grid=(batch, heads, nq // 2),
        in_specs=[
            pl.BlockSpec((None, 4, HEAD_DIM), lambda b, h, i: (j, 0, 0)),
            pl.BlockSpec((None, 1, vd), lambda b, h, i: (j, 0, 0)),
            q_spec(0, False), q_spec(1, False), q_spec(0, True), q_spec(1, True),
            pl.BlockSpec((None, None, seq, HEAD_DIM), lambda b, h, i: (b, h, 0, 0)),
            pl.BlockSpec((None, None, seq, HEAD_DIM), lambda b, h, i: (b, heads + h, 0, 0)),
            pl.BlockSpec((None, nk, None, vd, t), lambda b, h, i: (b, 0, h, 0, 0)),
        ],
        out_specs=pl.BlockSpec((None, 2, None, t, vd), lambda b, h, i: (b, 0, i, 0, h)),
        out_shape=jax.ShapeDtypeStruct((batch, 2, nq // 2, t, heads * vd), BF16),
        scratch_shapes=[pltpu.VMEM((2, vd, t), F32)],
        compiler_params=_params("parallel", "parallel", "parallel"),
        name="diff_attn",
    )(b_lambda, subln_g, q, q, q, q, k, k, vt)


def _out_proj_kernel(x_ref, mix_ref, mo_ref, w_ref, o_ref):
    mix_w = mix_ref.shape[1]
    tn = PROJ_TN
    for c in range(o_ref.shape[1] // tn):
        cols = slice(c * tn, (c + 1) * tn)
        y = (jnp.dot(mix_ref[...], w_ref[:mix_w, cols], preferred_element_type=F32)
             + jnp.dot(mo_ref[...], w_ref[mix_w:, cols], preferred_element_type=F32))
        o_ref[:, cols] = x_ref[:, cols] + y


def _out_proj(x, mix, mo, w_out, l):
    n, d = x.shape
    mix_w = mix.shape[-1]
    tm = OUT_TM
    if mix.ndim == 2:
        mix_spec = pl.BlockSpec((tm, mix_w), lambda r: (r, 0))
    else:
        _, _, half, t, _ = mix.shape
        assert t == tm

        def folded_block(r):
            b, qb = r // (2 * half), r % (2 * half)
            return (b, qb // half, jnp.where(qb < half, qb, 2 * half - 1 - qb), 0, 0)

        mix_spec = pl.BlockSpec((None, None, None, tm, mix_w), folded_block)
    return pl.pallas_call(
        _out_proj_kernel,
        grid=(n // tm,),
        in_specs=[
            pl.BlockSpec((tm, d), lambda r: (r, 0)),
            mix_spec,
            pl.BlockSpec((tm, MEM_W), lambda r: (r, 0)),
            pl.BlockSpec((None, d, d), lambda r: (l, 0, 0)),
        ],
        out_specs=pl.BlockSpec((tm, d), lambda r: (r, 0)),
        out_shape=jax.ShapeDtypeStruct((n, d), F32),
        compiler_params=_params("parallel"),
        name="out_proj",
    )(x, mix, mo, w_out)


def kernel(x, mem, norm_g, ffn_w_gu, ffn_w_down, w_out, mem_norm_g, mem_w_kv, mem_q_norm_g,
           mem_k_norm_g, a_w_in, a_v_norm_g, a_w_s, a_b_s, kv_norm_g, w_kv, k_norm_g, b_w_in,
           b_q_norm_g, b_lambda, b_subln_g):
    batch, seq, d = x.shape
    depth = norm_g.shape[0]
    n_a = a_w_in.shape[0]
    mix_w = d - MEM_W
    n = batch * seq

    w_out, mem_w_kv, a_w_in, w_kv, b_w_in = (
        w.astype(BF16) for w in (w_out, mem_w_kv, a_w_in, w_kv, b_w_in))
    ffn_w = (ffn_w_gu[0, 0].astype(BF16), ffn_w_down[0, 0].astype(BF16))

    def ffn(x, l, i, ffn_w):
        last = (l == depth - 1 and i == 1)
        nxt = None if last else (ffn_w_gu, ffn_w_down) + ((l, 1) if i == 0 else (l + 1, 0))
        res = _ffn(x, norm_g, l, 2 * i, ffn_w[0], ffn_w[1], nxt)
        return (res[0], None) if last else (res[0], (res[1], res[2]))

    norm_g = norm_g[:, :, None, :]
    mem_norm_g = mem_norm_g[:, None, :]
    mem_q_norm_g = mem_q_norm_g[:, None, :]
    mem_k_norm_g = mem_k_norm_g[:, None, :]
    a_v_norm_g = a_v_norm_g[:, None, :]
    b_q_norm_g = b_q_norm_g[:, None, :]
    b_subln_g = b_subln_g[:, None, :]
    a_b_s_t = jnp.swapaxes(a_b_s, 1, 2)

    pos = jnp.arange(seq, dtype=F32)
    inv = ROPE_THETA ** (-jnp.arange(0, HEAD_DIM, 2, dtype=F32) / HEAD_DIM)
    ang = pos[:, None] * inv[None, :]
    ang = jnp.concatenate([ang, ang], axis=-1)
    cos = jnp.cos(ang)
    sign = jnp.where(jnp.arange(HEAD_DIM) < HEAD_DIM // 2, -1.0, 1.0).astype(F32)
    sin_signed = jnp.sin(ang) * sign[None, :]

    x = x.reshape(n, d)
    mk, mv = _mem_kv(mem.reshape(batch * mem.shape[1], d), mem_norm_g, mem_w_kv, mem_k_norm_g)

    k_sh = v_sh = None
    for l in range(depth):
        if l == n_a:
            k_sh, v_sh = _kv_proj(x, kv_norm_g[None, :], w_kv, k_norm_g[None, :], cos, sin_signed,
                                  batch, seq, mix_w, b_subln_g.shape[-1])
        x, ffn_w = ffn(x, l, 0, ffn_w)
        if l < n_a:
            mix, mo = _mixer_a(x, norm_g, l, a_w_in, a_v_norm_g, a_w_s, a_b_s_t, mem_q_norm_g,
                               mk, mv, seq)
        else:
            j = l - n_a
            q, mo = _proj_b(x, norm_g, l, b_w_in, j, b_q_norm_g, cos, sin_signed, mem_q_norm_g,
                            mk, mv, seq)
            lam_init = 0.8 - 0.6 * math.exp(-0.3 * l)
            mix = _diff_attn(q, k_sh, v_sh, b_lambda, b_subln_g, j, lam_init, batch, seq)
        x = _out_proj(x, mix, mo, w_out, l)
        x, ffn_w = ffn(x, l, 1, ffn_w)
    return x.reshape(batch, seq, d)
```

```python
import functools
import math

import jax
import jax.numpy as jnp
from jax import lax
from jax.experimental import pallas as pl
from jax.experimental.pallas import tpu as pltpu

EPS = 1e-6
HEAD_DIM = 128
MEM_HEADS = 4
MEM_W = MEM_HEADS * HEAD_DIM
CHUNK = 128
GMLP_GROUPS = 6
ROPE_THETA = 10000.0
ATTN_SCALE = HEAD_DIM ** -0.5
SQRT_HALF = 0.7071067811865476

V7X_VMEM_BYTES = 64 * 1024 * 1024
VMEM_LIMIT_BYTES = V7X_VMEM_BYTES - 8 * 1024 * 1024
FFN_VMEM_LIMIT_BYTES = V7X_VMEM_BYTES - 5 * 1024 * 1024
FFN_TM = 1024
FFN_TF = 512
PROJ_TM = 512
PROJ_B_TM = 256
PROJ_SUB = 256
PROJ_TN = 512
OUT_TM = 512
ATTN_T = 512
ATTN_UNROLL = 4
V_ONES = 16
LOG2E = 1.4426950408889634
Q_PRESCALE = ATTN_SCALE * LOG2E

F32 = jnp.float32
BF16 = jnp.bfloat16
NT_DIMS = (((1,), (1,)), ((), ()))


def _params(*semantics):
    return pltpu.CompilerParams(dimension_semantics=semantics,
                                vmem_limit_bytes=VMEM_LIMIT_BYTES)


def _rms(x, g):
    return x * lax.rsqrt(jnp.mean(x * x, axis=-1, keepdims=True) + EPS) * g


def _gelu(z):
    return 0.5 * z * (1.0 + lax.erf(z * SQRT_HALF))


def _rope(r, cos, sin_signed):
    return r * cos + pltpu.roll(r, HEAD_DIM // 2, 1) * sin_signed


def _mem_attn(zq, gq, mk_ref, mv_ref, mo_ref, rows):
    for hh in range(MEM_HEADS):
        cols = slice(hh * HEAD_DIM, (hh + 1) * HEAD_DIM)
        qh = _rms(zq[:, cols], gq).astype(BF16)
        s = lax.dot_general(qh, mk_ref[:, cols], NT_DIMS, preferred_element_type=F32) * ATTN_SCALE
        e = jnp.exp(s - jnp.max(s, axis=-1, keepdims=True))
        p = e / jnp.sum(e, axis=-1, keepdims=True)
        mo_ref[rows, cols] = jnp.dot(p.astype(BF16), mv_ref[:, cols],
                                     preferred_element_type=F32).astype(BF16)


def _sub_blocks(tm):
    return [slice(s, s + PROJ_SUB) for s in range(0, tm, PROJ_SUB)]


def _ffn_kernel(x_ref, g_ref, wg_ref, wu_ref, wd_ref, *refs):
    n_cast = (len(refs) - 2) // 2
    cast_in, o_ref, cast_out, h_ref = refs[:n_cast], refs[n_cast], refs[n_cast + 1:-1], refs[-1]
    j = pl.program_id(1)

    def step(first):
        if first:
            h_ref[...] = _rms(x_ref[...], g_ref[...]).astype(BF16)
        h = h_ref[...]
        g = jnp.dot(h, wg_ref[...], preferred_element_type=F32)
        u = jnp.dot(h, wu_ref[...], preferred_element_type=F32)
        for src_ref, dst_ref in zip(cast_in, cast_out):
            dst_ref[...] = src_ref[...].astype(BF16)
        act = (0.5 * g * jax.nn.sigmoid(g)) * u
        down = jnp.dot(act.astype(BF16), wd_ref[...], preferred_element_type=F32)
        if first:
            o_ref[...] = x_ref[...] + down
        else:
            o_ref[...] += down

    pl.when(j == 0)(functools.partial(step, True))
    pl.when(j > 0)(functools.partial(step, False))


def _ffn(x, norm_g, l, slot, w_gu, w_down, next_w=None):
    n, d = x.shape
    f = w_down.shape[0]
    tm, tf = FFN_TM, FFN_TF
    nr, nf = n // tm, f // tf
    in_specs = [
        pl.BlockSpec((tm, d), lambda r, j: (r, 0)),
        pl.BlockSpec((None, None, 1, d), lambda r, j: (l, slot, 0, 0)),
        pl.BlockSpec((d, tf), lambda r, j: (0, j)),
        pl.BlockSpec((d, tf), lambda r, j: (0, j + nf)),
        pl.BlockSpec((tf, d), lambda r, j: (j, 0)),
    ]
    out_specs = [pl.BlockSpec((tm, d), lambda r, j: (r, 0))]
    out_shape = [jax.ShapeDtypeStruct((n, d), F32)]
    args = [x, norm_g, w_gu, w_gu, w_down]
    if next_w is not None:
        gu32, down32, l2, i2 = next_w
        steps = nr * nf
        assert d % nr == 0 and (2 * f) % nf == 0 and f % steps == 0
        in_specs += [
            pl.BlockSpec((None, None, d // nr, 2 * f // nf), lambda r, j: (l2, i2, r, j)),
            pl.BlockSpec((None, None, f // steps, d), lambda r, j: (l2, i2, r * nf + j, 0)),
        ]
        out_specs += [pl.BlockSpec((d // nr, 2 * f // nf), lambda r, j: (r, j)),
                      pl.BlockSpec((f // steps, d), lambda r, j: (r * nf + j, 0))]
        out_shape += [jax.ShapeDtypeStruct((d, 2 * f), BF16), jax.ShapeDtypeStruct((f, d), BF16)]
        args += [gu32, down32]
    return pl.pallas_call(
        _ffn_kernel,
        grid=(nr, nf),
        in_specs=in_specs,
        out_specs=out_specs,
        out_shape=out_shape,
        scratch_shapes=[pltpu.VMEM((tm, d), BF16)],
        compiler_params=pltpu.CompilerParams(dimension_semantics=("parallel", "arbitrary"),
                                             vmem_limit_bytes=FFN_VMEM_LIMIT_BYTES),
        name="ffn",
    )(*args)


def _mem_kv_kernel(mem_ref, g_ref, w_ref, gk_ref, mk_ref, mv_ref):
    h = _rms(mem_ref[...], g_ref[...]).astype(BF16)
    kv = jnp.dot(h, w_ref[...], preferred_element_type=F32)
    gk = gk_ref[...]
    for hh in range(MEM_HEADS):
        cols = slice(hh * HEAD_DIM, (hh + 1) * HEAD_DIM)
        mk_ref[:, cols] = _rms(kv[:, cols], gk).astype(BF16)
    mv_ref[...] = kv[:, MEM_W:].astype(BF16)


def _mem_kv(mem2, mem_norm_g, mem_w_kv, mem_k_norm_g):
    rows, d = mem2.shape
    depth = mem_w_kv.shape[0]
    out = jax.ShapeDtypeStruct((depth, rows, MEM_W), BF16)
    return pl.pallas_call(
        _mem_kv_kernel,
        grid=(depth,),
        in_specs=[
            pl.BlockSpec((rows, d), lambda l: (0, 0)),
            pl.BlockSpec((None, 1, d), lambda l: (l, 0, 0)),
            pl.BlockSpec((None, d, 2 * MEM_W), lambda l: (l, 0, 0)),
            pl.BlockSpec((None, 1, HEAD_DIM), lambda l: (l, 0, 0)),
        ],
        out_specs=[pl.BlockSpec((None, rows, MEM_W), lambda l: (l, 0, 0))] * 2,
        out_shape=[out, out],
        compiler_params=_params("parallel"),
        name="mem_kv",
    )(mem2, mem_norm_g, mem_w_kv, mem_k_norm_g)


def _mixer_a_kernel(x_ref, g_ref, w_ref, gv_ref, ws_ref, bs_ref, gq_ref, mk_ref, mv_ref,
                    mix_ref, mo_ref, h_ref, u_ref, v_ref):
    tm = x_ref.shape[0]
    mix_w = u_ref.shape[1]
    tn = PROJ_TN
    gw = mix_w // GMLP_GROUPS
    causal = (lax.broadcasted_iota(jnp.int32, (CHUNK, CHUNK), 0)
              >= lax.broadcasted_iota(jnp.int32, (CHUNK, CHUNK), 1))
    w_s = [jnp.where(causal, ws_ref[grp], 0.0).astype(BF16) for grp in range(GMLP_GROUPS)]
    h_ref[...] = _rms(x_ref[...], g_ref[...]).astype(BF16)
    for c in range(mix_w // tn):
        cols = slice(c * tn, (c + 1) * tn)
        u_ref[:, cols] = _gelu(jnp.dot(h_ref[...], w_ref[:, cols], preferred_element_type=F32))
        v_ref[:, cols] = _gelu(jnp.dot(h_ref[...], w_ref[:, mix_w + c * tn:mix_w + (c + 1) * tn],
                                       preferred_element_type=F32))
    zq = jnp.dot(h_ref[...], w_ref[:, 2 * mix_w:], preferred_element_type=F32)
    _mem_attn(zq, gq_ref[...], mk_ref, mv_ref, mo_ref, slice(0, tm))
    for t in range(0, tm, CHUNK):
        chunk = slice(t, t + CHUNK)
        vn = _rms(v_ref[chunk, :], gv_ref[...]).astype(BF16)
        for grp in range(GMLP_GROUPS):
            cols = slice(grp * gw, (grp + 1) * gw)
            mixed = (jnp.dot(w_s[grp], vn[:, cols], preferred_element_type=F32)
                     + bs_ref[:, grp:grp + 1])
            mix_ref[chunk, cols] = (u_ref[chunk, cols] * mixed).astype(BF16)


def _mixer_a(x, norm_g, l, w_in, v_norm_g, w_s, b_s_t, mem_q_norm_g, mk, mv, seq):
    n, d = x.shape
    mix_w = v_norm_g.shape[-1]
    tm = PROJ_TM
    blocks_per_seq = seq // tm
    mem_len = mk.shape[1] // (n // seq)
    mem_spec = pl.BlockSpec((None, mem_len, MEM_W), lambda r: (l, r // blocks_per_seq, 0))
    return pl.pallas_call(
        _mixer_a_kernel,
        grid=(n // tm,),
        in_specs=[
            pl.BlockSpec((tm, d), lambda r: (r, 0)),
            pl.BlockSpec((None, None, 1, d), lambda r: (l, 1, 0, 0)),
            pl.BlockSpec((None, d, 2 * mix_w + MEM_W), lambda r: (l, 0, 0)),
            pl.BlockSpec((None, 1, mix_w), lambda r: (l, 0, 0)),
            pl.BlockSpec((None, GMLP_GROUPS, CHUNK, CHUNK), lambda r: (l, 0, 0, 0)),
            pl.BlockSpec((None, CHUNK, GMLP_GROUPS), lambda r: (l, 0, 0)),
            pl.BlockSpec((None, 1, HEAD_DIM), lambda r: (l, 0, 0)),
            mem_spec, mem_spec,
        ],
        out_specs=[pl.BlockSpec((tm, mix_w), lambda r: (r, 0)),
                   pl.BlockSpec((tm, MEM_W), lambda r: (r, 0))],
        out_shape=[jax.ShapeDtypeStruct((n, mix_w), BF16),
                   jax.ShapeDtypeStruct((n, MEM_W), BF16)],
        scratch_shapes=[pltpu.VMEM((tm, d), BF16),
                        pltpu.VMEM((tm, mix_w), F32),
                        pltpu.VMEM((tm, mix_w), F32)],
        compiler_params=_params("parallel"),
        name="mixer_a",
    )(x, norm_g, w_in, v_norm_g, w_s, b_s_t, mem_q_norm_g, mk, mv)


def _kv_proj_kernel(x_ref, g_ref, w_ref, gk_ref, cos_ref, sin_ref, k_ref, vt_ref, h_ref, v_ref):
    qk_w = k_ref.shape[0] * HEAD_DIM
    heads, _, tm = vt_ref.shape
    vd = v_ref.shape[1]
    tn = PROJ_TN
    gk = gk_ref[...]
    for rows in _sub_blocks(tm):
        h_ref[rows, :] = _rms(x_ref[rows, :], g_ref[...]).astype(BF16)
    for rows in _sub_blocks(tm):
        cos, sin = cos_ref[rows, :], sin_ref[rows, :]
        for c in range(qk_w // tn):
            z = jnp.dot(h_ref[rows, :], w_ref[:, c * tn:(c + 1) * tn], preferred_element_type=F32)
            for k in range(tn // HEAD_DIM):
                r = _rms(z[:, k * HEAD_DIM:(k + 1) * HEAD_DIM], gk)
                k_ref[c * (tn // HEAD_DIM) + k, rows, :] = _rope(r, cos, sin).astype(BF16)
        for hh in range(heads):
            v_ref[rows, :] = jnp.dot(h_ref[rows, :], w_ref[:, qk_w + hh * vd:qk_w + (hh + 1) * vd],
                                     preferred_element_type=F32)
            vt_ref[hh, :vd, rows] = v_ref[rows, :].T.astype(BF16)
            vt_ref[hh, vd:, rows] = jnp.ones((V_ONES, rows.stop - rows.start), BF16)


def _kv_proj(x, kv_norm_g, w_kv, k_norm_g, cos, sin_signed, batch, seq, qk_w, vd):
    n, d = x.shape
    heads = (w_kv.shape[1] - qk_w) // vd
    tm = ATTN_T
    blocks_per_seq = seq // tm
    rope_spec = pl.BlockSpec((tm, HEAD_DIM), lambda r: (r % blocks_per_seq, 0))
    return pl.pallas_call(
        _kv_proj_kernel,
        grid=(n // tm,),
        in_specs=[
            pl.BlockSpec((tm, d), lambda r: (r, 0)),
            pl.BlockSpec((1, d), lambda r: (0, 0)),
            pl.BlockSpec((d, qk_w + heads * vd), lambda r: (0, 0)),
            pl.BlockSpec((1, HEAD_DIM), lambda r: (0, 0)),
            rope_spec, rope_spec,
        ],
        out_specs=[pl.BlockSpec((None, qk_w // HEAD_DIM, tm, HEAD_DIM),
                                lambda r: (r // blocks_per_seq, 0, r % blocks_per_seq, 0)),
                   pl.BlockSpec((None, None, heads, vd + V_ONES, tm),
                                lambda r: (r // blocks_per_seq, r % blocks_per_seq, 0, 0, 0))],
        out_shape=[jax.ShapeDtypeStruct((batch, qk_w // HEAD_DIM, seq, HEAD_DIM), BF16),
                   jax.ShapeDtypeStruct((batch, blocks_per_seq, heads, vd + V_ONES, tm), BF16)],
        scratch_shapes=[pltpu.VMEM((tm, d), BF16), pltpu.VMEM((tm, vd), F32)],
        compiler_params=_params("parallel"),
        name="kv_proj",
    )(x, kv_norm_g, w_kv, k_norm_g, cos, sin_signed)


def _proj_b_kernel(x_ref, g_ref, w_ref, gqn_ref, cos_ref, sin_ref, gq_ref, mk_ref, mv_ref,
                   q_ref, mo_ref, h_ref):
    n_qk, tm, _ = q_ref.shape
    qk_w = n_qk * HEAD_DIM
    tn = PROJ_TN
    gqn = gqn_ref[...]
    for rows in _sub_blocks(tm):
        h_ref[rows, :] = _rms(x_ref[rows, :], g_ref[...]).astype(BF16)
    for rows in _sub_blocks(tm):
        cos, sin = cos_ref[rows, :], sin_ref[rows, :]
        for c in range(qk_w // tn):
            z = jnp.dot(h_ref[rows, :], w_ref[:, c * tn:(c + 1) * tn], preferred_element_type=F32)
            for k in range(tn // HEAD_DIM):
                r = _rms(z[:, k * HEAD_DIM:(k + 1) * HEAD_DIM], gqn)
                q_ref[c * (tn // HEAD_DIM) + k, rows, :] = (
                    _rope(r, cos, sin) * Q_PRESCALE).astype(BF16)
        zq = jnp.dot(h_ref[rows, :], w_ref[:, qk_w:], preferred_element_type=F32)
        _mem_attn(zq, gq_ref[...], mk_ref, mv_ref, mo_ref, rows)


def _proj_b(x, norm_g, l, w_in, j, q_norm_g, cos, sin_signed, mem_q_norm_g, mk, mv, seq):
    n, d = x.shape
    qk_w = w_in.shape[2] - MEM_W
    tm = PROJ_B_TM
    blocks_per_seq = seq // tm
    mem_len = mk.shape[1] // (n // seq)
    rope_spec = pl.BlockSpec((tm, HEAD_DIM), lambda r: (r % blocks_per_seq, 0))
    mem_spec = pl.BlockSpec((None, mem_len, MEM_W), lambda r: (l, r // blocks_per_seq, 0))
    return pl.pallas_call(
        _proj_b_kernel,
        grid=(n // tm,),
        in_specs=[
            pl.BlockSpec((tm, d), lambda r: (r, 0)),
            pl.BlockSpec((None, None, 1, d), lambda r: (l, 1, 0, 0)),
            pl.BlockSpec((None, d, qk_w + MEM_W), lambda r: (j, 0, 0)),
            pl.BlockSpec((None, 1, HEAD_DIM), lambda r: (j, 0, 0)),
            rope_spec, rope_spec,
            pl.BlockSpec((None, 1, HEAD_DIM), lambda r: (l, 0, 0)),
            mem_spec, mem_spec,
        ],
        out_specs=[pl.BlockSpec((None, qk_w // HEAD_DIM, tm, HEAD_DIM),
                                lambda r: (r // blocks_per_seq, 0, r % blocks_per_seq, 0)),
                   pl.BlockSpec((tm, MEM_W), lambda r: (r, 0))],
        out_shape=[jax.ShapeDtypeStruct((n // seq, qk_w // HEAD_DIM, seq, HEAD_DIM), BF16),
                   jax.ShapeDtypeStruct((n, MEM_W), BF16)],
        scratch_shapes=[pltpu.VMEM((tm, d), BF16)],
        compiler_params=_params("parallel"),
        name="proj_b",
    )(x, norm_g, w_in, q_norm_g, cos, sin_signed, mem_q_norm_g, mk, mv)


def _diff_attn_kernel(lam_ref, gs_ref, q0a_ref, q1a_ref, q0b_ref, q1b_ref, k0_ref, k1_ref, vt_ref,
                      o_ref, acc_ref, *, lam_init):
    t = ATTN_T
    k_refs = (k0_ref, k1_ref)
    lp = lam_ref[...]
    lam = (jnp.exp(jnp.sum(lp[0:1] * lp[1:2], axis=-1, keepdims=True))
           - jnp.exp(jnp.sum(lp[2:3] * lp[3:4], axis=-1, keepdims=True)) + lam_init)

    def query_block(qi, q_refs, out_slot):
        acc_ref[...] = jnp.zeros(acc_ref.shape, F32)

        def scores(ki, c):
            rows = pl.ds(pl.multiple_of(ki * t, t), t)
            return lax.dot_general(k_refs[c][rows, :], q_refs[c][...], NT_DIMS,
                                   preferred_element_type=F32)

        def update(ki, c, st, stat, masked):
            m_old = stat
            if masked:
                key_le_query = (lax.broadcasted_iota(jnp.int32, (t, t), 0)
                                <= lax.broadcasted_iota(jnp.int32, (t, t), 1))
                st = jnp.where(key_le_query, st, -jnp.inf)
            m_new = jnp.maximum(m_old, jnp.max(st, axis=0, keepdims=True))
            alpha = jnp.exp2(m_old - m_new)
            pt = jnp.exp2(st - m_new)
            acc_ref[c] = alpha * acc_ref[c] + jnp.dot(vt_ref[ki], pt.astype(BF16),
                                                      preferred_element_type=F32)
            return m_new

        def blocks(kis, stats, masked_last):
            stats = list(stats)
            work = [(ki, c, masked_last and n == len(kis) - 1)
                    for n, ki in enumerate(kis) for c in range(2)]
            pending = [scores(ki, c) for ki, c, _ in work[:2]]
            for n, (ki, c, masked) in enumerate(work):
                if n + 2 < len(work):
                    pending.append(scores(work[n + 2][0], work[n + 2][1]))
                stats[c] = update(ki, c, pending[n], stats[c], masked)
            return tuple(stats)

        init = (jnp.full((1, t), -jnp.inf, F32),) * 2
        u = ATTN_UNROLL
        stats = lax.fori_loop(
            0, qi // u, lambda j, s: blocks(tuple(u * j + n for n in range(u)), s, False), init)
        base = (qi // u) * u
        stats = lax.switch(
            qi % u,
            [functools.partial(lambda r, s: blocks(tuple(base + n for n in range(r)) + (qi,), s, True), r)
             for r in range(u)],
            stats)
        vd = acc_ref.shape[1] - V_ONES
        l0, l1 = acc_ref[0, vd:vd + 1, :], acc_ref[1, vd:vd + 1, :]
        ot = acc_ref[0, :vd, :] / l0 - lam * (acc_ref[1, :vd, :] / l1)
        ot = ot * lax.rsqrt(jnp.mean(ot * ot, axis=0, keepdims=True) + EPS)
        o_ref[out_slot] = (ot.T * gs_ref[...] * (1.0 - lam_init)).astype(BF16)

    i = pl.program_id(2)
    query_block(i, (q0a_ref, q1a_ref), 0)
    query_block(2 * pl.num_programs(2) - 1 - i, (q0b_ref, q1b_ref), 1)


def _diff_attn(q, k, vt, b_lambda, subln_g, j, lam_init, batch, seq):
    heads = q.shape[1] // 2
    _, nk, _, va, t = vt.shape
    vd = va - V_ONES
    nq = seq // t
    assert nq % 2 == 0

    def q_spec(c, mirror):
        return pl.BlockSpec((None, None, t, HEAD_DIM),
                            lambda b, h, i: (b, c * heads + h, nq - 1 - i if mirror else i, 0))

    return pl.pallas_call(
        functools.partial(_diff_attn_kernel, lam_init=lam_init),
        grid=(batch, heads, nq // 2),
        in_specs=[
            pl.BlockSpec((None, 4, HEAD_DIM), lambda b, h, i: (j, 0, 0)),
            pl.BlockSpec((None, 1, vd), lambda b, h, i: (j, 0, 0)),
            q_spec(0, False), q_spec(1, False), q_spec(0, True), q_spec(1, True),
            pl.BlockSpec((None, None, seq, HEAD_DIM), lambda b, h, i: (b, h, 0, 0)),
            pl.BlockSpec((None, None, seq, HEAD_DIM), lambda b, h, i: (b, heads + h, 0, 0)),
            pl.BlockSpec((None, nk, None, va, t), lambda b, h, i: (b, 0, h, 0, 0)),
        ],
        out_specs=pl.BlockSpec((None, 2, None, t, vd), lambda b, h, i: (b, 0, i, 0, h)),
        out_shape=jax.ShapeDtypeStruct((batch, 2, nq // 2, t, heads * vd), BF16),
        scratch_shapes=[pltpu.VMEM((2, va, t), F32)],
        compiler_params=_params("parallel", "parallel", "parallel"),
        name="diff_attn",
    )(b_lambda, subln_g, q, q, q, q, k, k, vt)


def _out_proj_kernel(x_ref, mix_ref, mo_ref, w_ref, o_ref):
    mix_w = mix_ref.shape[1]
    tn = PROJ_TN
    for c in range(o_ref.shape[1] // tn):
        cols = slice(c * tn, (c + 1) * tn)
        y = (jnp.dot(mix_ref[...], w_ref[:mix_w, cols], preferred_element_type=F32)
             + jnp.dot(mo_ref[...], w_ref[mix_w:, cols], preferred_element_type=F32))
        o_ref[:, cols] = x_ref[:, cols] + y


def _out_proj(x, mix, mo, w_out, l):
    n, d = x.shape
    mix_w = mix.shape[-1]
    tm = OUT_TM
    if mix.ndim == 2:
        mix_spec = pl.BlockSpec((tm, mix_w), lambda r: (r, 0))
    else:
        _, _, half, t, _ = mix.shape
        assert t == tm

        def folded_block(r):
            b, qb = r // (2 * half), r % (2 * half)
            return (b, qb // half, jnp.where(qb < half, qb, 2 * half - 1 - qb), 0, 0)

        mix_spec = pl.BlockSpec((None, None, None, tm, mix_w), folded_block)
    return pl.pallas_call(
        _out_proj_kernel,
        grid=(n // tm,),
        in_specs=[
            pl.BlockSpec((tm, d), lambda r: (r, 0)),
            mix_spec,
            pl.BlockSpec((tm, MEM_W), lambda r: (r, 0)),
            pl.BlockSpec((None, d, d), lambda r: (l, 0, 0)),
        ],
        out_specs=pl.BlockSpec((tm, d), lambda r: (r, 0)),
        out_shape=jax.ShapeDtypeStruct((n, d), F32),
        compiler_params=_params("parallel"),
        name="out_proj",
    )(x, mix, mo, w_out)


def kernel(x, mem, norm_g, ffn_w_gu, ffn_w_down, w_out, mem_norm_g, mem_w_kv, mem_q_norm_g,
           mem_k_norm_g, a_w_in, a_v_norm_g, a_w_s, a_b_s, kv_norm_g, w_kv, k_norm_g, b_w_in,
           b_q_norm_g, b_lambda, b_subln_g):
    batch, seq, d = x.shape
    depth = norm_g.shape[0]
    n_a = a_w_in.shape[0]
    mix_w = d - MEM_W
    n = batch * seq

    w_out, mem_w_kv, a_w_in, w_kv, b_w_in = (
        w.astype(BF16) for w in (w_out, mem_w_kv, a_w_in, w_kv, b_w_in))
    ffn_w = (ffn_w_gu[0, 0].astype(BF16), ffn_w_down[0, 0].astype(BF16))

    def ffn(x, l, i, ffn_w):
        last = (l == depth - 1 and i == 1)
        nxt = None if last else (ffn_w_gu, ffn_w_down) + ((l, 1) if i == 0 else (l + 1, 0))
        res = _ffn(x, norm_g, l, 2 * i, ffn_w[0], ffn_w[1], nxt)
        return (res[0], None) if last else (res[0], (res[1], res[2]))

    norm_g = norm_g[:, :, None, :]
    mem_norm_g = mem_norm_g[:, None, :]
    mem_q_norm_g = mem_q_norm_g[:, None, :]
    mem_k_norm_g = mem_k_norm_g[:, None, :]
    a_v_norm_g = a_v_norm_g[:, None, :]
    b_q_norm_g = b_q_norm_g[:, None, :]
    b_subln_g = b_subln_g[:, None, :]
    a_b_s_t = jnp.swapaxes(a_b_s, 1, 2)

    pos = jnp.arange(seq, dtype=F32)
    inv = ROPE_THETA ** (-jnp.arange(0, HEAD_DIM, 2, dtype=F32) / HEAD_DIM)
    ang = pos[:, None] * inv[None, :]
    ang = jnp.concatenate([ang, ang], axis=-1)
    cos = jnp.cos(ang)
    sign = jnp.where(jnp.arange(HEAD_DIM) < HEAD_DIM // 2, -1.0, 1.0).astype(F32)
    sin_signed = jnp.sin(ang) * sign[None, :]

    x = x.reshape(n, d)
    mk, mv = _mem_kv(mem.reshape(batch * mem.shape[1], d), mem_norm_g, mem_w_kv, mem_k_norm_g)

    k_sh = v_sh = None
    for l in range(depth):
        if l == n_a:
            k_sh, v_sh = _kv_proj(x, kv_norm_g[None, :], w_kv, k_norm_g[None, :], cos, sin_signed,
                                  batch, seq, mix_w, b_subln_g.shape[-1])
        x, ffn_w = ffn(x, l, 0, ffn_w)
        if l < n_a:
            mix, mo = _mixer_a(x, norm_g, l, a_w_in, a_v_norm_g, a_w_s, a_b_s_t, mem_q_norm_g,
                               mk, mv, seq)
        else:
            j = l - n_a
            q, mo = _proj_b(x, norm_g, l, b_w_in, j, b_q_norm_g, cos, sin_signed, mem_q_norm_g,
                            mk, mv, seq)
            lam_init = 0.8 - 0.6 * math.exp(-0.3 * l)
            mix = _diff_attn(q, k_sh, v_sh, b_lambda, b_subln_g, j, lam_init, batch, seq)
        x = _out_proj(x, mix, mo, w_out, l)
        x, ffn_w = ffn(x, l, 1, ffn_w)
    return x.reshape(batch, seq, d)
```
